```python
import math
import jax, jax.numpy as jnp
from jax import lax
import numpy as np

D_MODEL = 1024
BATCH = 8
SEQ = 2048
DEPTH = 2
DEC_BATCH = 32
DEC_SEQ = 4
PAST_LEN = 8192
PAGE_SIZE = 128

MIX_W = D_MODEL
ATTN_W = MIX_W // 2
POOL_W = MIX_W - ATTN_W
HEAD_DIM = 64
N_HEADS = ATTN_W // HEAD_DIM
MOBA_BLOCK = 256
MOBA_TOPK = 3
ATTN_QBLOCK = 64
POOL_WINDOWS = (2, 4, 8, 16)
POOL_GROUPS = len(POOL_WINDOWS)
POOL_GW = POOL_W // POOL_GROUPS
POOL_BUF = max(POOL_WINDOWS) - 1
PEER_HEADS = 8
PEER_NKEYS = 128
PEER_EXPERTS = PEER_NKEYS * PEER_NKEYS
PEER_DK = 256
PEER_HALF = PEER_DK // 2
PEER_TOPK = 16
PEER_TBLOCK = 128
RMS_EPS = 1e-6
NEG_INF = -1e30

kernel_name = "hymba_moba_pool_peer_decode_step"


def _rmsnorm(x, g):
    xf = x.astype(jnp.float32)
    y = xf * lax.rsqrt(jnp.mean(xf * xf, axis=-1, keepdims=True) + RMS_EPS) * g.astype(jnp.float32)
    return y.astype(x.dtype)


def _split_proj(proj):
    lead = proj.shape[:-1]
    hd = lead + (N_HEADS, HEAD_DIM)
    q = proj[..., :ATTN_W].reshape(hd)
    k = proj[..., ATTN_W:2 * ATTN_W].reshape(hd)
    v = proj[..., 2 * ATTN_W:3 * ATTN_W].reshape(hd)
    p = proj[..., 3 * ATTN_W:]
    return q, k, v, p


def _to_blocks(k, v):
    b, L, h, dh = k.shape
    nb = -(-L // MOBA_BLOCK)
    padw = ((0, 0), (0, nb * MOBA_BLOCK - L), (0, 0), (0, 0))
    kb = jnp.pad(k, padw).reshape(b, nb, MOBA_BLOCK, h, dh).transpose(0, 3, 1, 2, 4)
    vb = jnp.pad(v, padw).reshape(b, nb, MOBA_BLOCK, h, dh).transpose(0, 3, 1, 2, 4)
    kbar = jnp.mean(kb.astype(jnp.float32), axis=3)
    return kb, vb, kbar


def _moba_query_block(q, t, kb, vb, kbar):
    nq, h, dh = q.shape
    nb = kb.shape[1]
    own = t // MOBA_BLOCK
    gate = jnp.einsum('qhd,hnd->qhn', q.astype(jnp.float32), kbar)
    past = jnp.arange(nb, dtype=jnp.int32)[None, None, :] < own[:, None, None]
    gate = jnp.where(past, gate, NEG_INF)
    _, top = lax.top_k(gate, min(MOBA_TOPK, nb))
    top = top.astype(jnp.int32)
    own_b = jnp.broadcast_to(own[:, None, None], (nq, h, 1)).astype(jnp.int32)
    sel = jnp.concatenate([top, own_b], axis=-1)
    valid = jnp.concatenate([top < own[:, None, None], jnp.ones((nq, h, 1), bool)], axis=-1)
    hidx = jnp.arange(h, dtype=jnp.int32)[None, :, None]
    kg = kb[hidx, sel]
    vg = vb[hidx, sel]
    s = jnp.einsum('qhd,qhjkd->qhjk', q, kg, preferred_element_type=jnp.float32) * (HEAD_DIM ** -0.5)
    kpos = sel[..., None] * MOBA_BLOCK + jnp.arange(MOBA_BLOCK, dtype=jnp.int32)
    mask = valid[..., None] & (kpos <= t[:, None, None, None])
    s = jnp.where(mask, s, NEG_INF)
    p = jax.nn.softmax(s.reshape(nq, h, -1), axis=-1).reshape(s.shape)
    return jnp.einsum('qhjk,qhjkd->qhd', p.astype(vg.dtype), vg)


def _moba_prompt(q, k, v):
    b, s, h, dh = q.shape
    kb, vb, kbar = _to_blocks(k, v)
    nqb = s // ATTN_QBLOCK
    qb = q.reshape(b * nqb, ATTN_QBLOCK, h, dh)
    flat = jnp.arange(b * nqb, dtype=jnp.int32)
    seq_idx = flat // nqb
    blk_idx = flat % nqb

    def step(args):
        qi, bi, ji = args
        t = ji * ATTN_QBLOCK + jnp.arange(ATTN_QBLOCK, dtype=jnp.int32)
        return _moba_query_block(qi, t, kb[bi], vb[bi], kbar[bi])

    o = lax.map(step, (qb, seq_idx, blk_idx))
    return o.reshape(b, s, h * dh)


def _moba_sample(q, k_new, v_new, k_past, v_past):
    db, tn, h, dh = q.shape
    past_len = k_past.shape[1]
    kb, vb, kbar = _to_blocks(jnp.concatenate([k_past, k_new], axis=1),
                              jnp.concatenate([v_past, v_new], axis=1))
    t = past_len + jnp.arange(tn, dtype=jnp.int32)
    o = lax.map(lambda a: _moba_query_block(a[0], t, a[1], a[2], a[3]), (q, kb, vb, kbar))
    return o.reshape(db, tn, h * dh)


def _pool_mix(pin, prefix, pos, w_pool, pool_scale):
    b, tn, _ = pin.shape
    xf = jnp.concatenate([prefix, pin], axis=1).astype(jnp.float32)
    c = jnp.concatenate([jnp.zeros((b, 1, POOL_W), jnp.float32), jnp.cumsum(xf, axis=1)], axis=1)
    end = c[:, POOL_BUF + 1:]
    cur = xf[:, POOL_BUF:]
    outs = []
    for g, w in enumerate(POOL_WINDOWS):
        sl = slice(g * POOL_GW, (g + 1) * POOL_GW)
        start = c[:, POOL_BUF + 1 - w:POOL_BUF + 1 - w + tn, sl]
        cnt = jnp.minimum(pos + 1, w).astype(jnp.float32)[None, :, None]
        outs.append((end[..., sl] - start) / cnt - cur[..., sl])
    r = jnp.stack(outs, axis=2)
    y = jnp.einsum('btgc,gcd->btgd', r, w_pool.astype(jnp.float32)).reshape(b, tn, POOL_W)
    return (y * pool_scale.astype(jnp.float32)).astype(pin.dtype)


def _peer_ffn(h, wq, sub_keys, u_tab, v_tab):
    lead = h.shape[:-1]
    hf = h.reshape(-1, D_MODEL)
    n_tok = hf.shape[0]
    n_pad = -(-n_tok // PEER_TBLOCK) * PEER_TBLOCK
    blocks = jnp.pad(hf, ((0, n_pad - n_tok), (0, 0))).reshape(-1, PEER_TBLOCK, D_MODEL)

    def step(hb):
        q = (hb @ wq).reshape(PEER_TBLOCK, PEER_HEADS, PEER_DK)
        s1 = jnp.einsum('thc,nc->thn', q[..., :PEER_HALF], sub_keys[0])
        s2 = jnp.einsum('thc,nc->thn', q[..., PEER_HALF:], sub_keys[1])
        v1, i1 = lax.top_k(s1, PEER_TOPK)
        v2, i2 = lax.top_k(s2, PEER_TOPK)
        cand = (v1[..., :, None] + v2[..., None, :]).reshape(PEER_TBLOCK, PEER_HEADS, PEER_TOPK * PEER_TOPK)
        sc, ci = lax.top_k(cand, PEER_TOPK)
        e = (jnp.take_along_axis(i1, ci // PEER_TOPK, axis=-1) * PEER_NKEYS
             + jnp.take_along_axis(i2, ci % PEER_TOPK, axis=-1))
        g = jax.nn.softmax(sc.astype(jnp.float32), axis=-1)
        ug = u_tab[e]
        vg = v_tab[e]
        a = jnp.einsum('td,thkd->thk', hb, ug, preferred_element_type=jnp.float32)
        wgt = (g * jax.nn.gelu(a, approximate=False)).astype(hb.dtype)
        return jnp.einsum('thk,thkd->td', wgt, vg)

    out = lax.map(step, blocks).reshape(n_pad, D_MODEL)[:n_tok]
    return out.reshape(lead + (D_MODEL,))


def setup_inputs(seed: int = 0) -> dict:
    key = jax.random.key(seed)
    ks = jax.random.split(key, 20)
    f32 = jnp.float32
    n_pages = PAST_LEN // PAGE_SIZE
    n_used = DEC_BATCH * n_pages
    n_pool = n_used + max(1, n_used // 4)

    def nrm(k, shape, s):
        return jax.random.normal(k, shape, f32) * s

    page_table = jax.random.permutation(ks[5], n_pool)[:n_used].reshape(DEC_BATCH, n_pages).astype(jnp.int32)
    return {
        "x_prompt": nrm(ks[0], (BATCH, SEQ, D_MODEL), 1.0),
        "x_sample": nrm(ks[1], (DEC_BATCH, DEC_SEQ, D_MODEL), 1.0),
        "cache_k": nrm(ks[2], (n_pool, DEPTH, PAGE_SIZE, N_HEADS, HEAD_DIM), 1.0),
        "cache_v": nrm(ks[3], (n_pool, DEPTH, PAGE_SIZE, N_HEADS, HEAD_DIM), 1.0),
        "state_pool": nrm(ks[4], (DEPTH, DEC_BATCH, POOL_BUF, POOL_W), 1.0),
        "page_table": page_table,
        "norm_mix": 1.0 + nrm(ks[6], (DEPTH, D_MODEL), 0.05),
        "w_in": nrm(ks[7], (DEPTH, D_MODEL, 3 * ATTN_W + POOL_W), D_MODEL ** -0.5),
        "w_pool": nrm(ks[8], (DEPTH, POOL_GROUPS, POOL_GW, POOL_GW), POOL_GW ** -0.5),
        "pool_scale": 1.0 + nrm(ks[9], (DEPTH, POOL_W), 0.1),
        "w_out": nrm(ks[10], (DEPTH, MIX_W, D_MODEL), MIX_W ** -0.5),
        "norm_ffn": 1.0 + nrm(ks[11], (DEPTH, D_MODEL), 0.05),
        "peer_wq": nrm(ks[12], (DEPTH, D_MODEL, PEER_HEADS * PEER_DK), D_MODEL ** -0.5),
        "peer_keys": nrm(ks[13], (DEPTH, 2, PEER_NKEYS, PEER_HALF), PEER_HALF ** -0.5),
        "peer_u": nrm(ks[14], (DEPTH, PEER_EXPERTS, D_MODEL), D_MODEL ** -0.5),
        "peer_v": nrm(ks[15], (DEPTH, PEER_EXPERTS, D_MODEL), PEER_HEADS ** -0.5),
        "norm_final": 1.0 + nrm(ks[16], (D_MODEL,), 0.05),
    }


def reference(x_prompt, x_sample, cache_k, cache_v, state_pool, page_table,
              norm_mix, w_in, w_pool, pool_scale, w_out, norm_ffn,
              peer_wq, peer_keys, peer_u, peer_v, norm_final):
    bp, sp, _ = x_prompt.shape
    db, ts, _ = x_sample.shape
    past_len = page_table.shape[1] * cache_k.shape[2]
    pos_p = jnp.arange(sp, dtype=jnp.int32)
    pos_s = past_len + jnp.arange(ts, dtype=jnp.int32)
    y_p, y_s = x_prompt, x_sample
    kp, vp, pp, kss, vss, pss = [], [], [], [], [], []
    for l in range(DEPTH):
        q, k, v, pin = _split_proj(_rmsnorm(y_p, norm_mix[l]) @ w_in[l])
        attn = _moba_prompt(q, k, v)
        pool = _pool_mix(pin, jnp.zeros((bp, POOL_BUF, POOL_W), pin.dtype), pos_p, w_pool[l], pool_scale[l])
        y_p = y_p + jnp.concatenate([attn, pool], axis=-1) @ w_out[l]
        y_p = y_p + _peer_ffn(_rmsnorm(y_p, norm_ffn[l]), peer_wq[l], peer_keys[l], peer_u[l], peer_v[l])
        kp.append(k)
        vp.append(v)
        pp.append(pin[:, -POOL_BUF:])
        q, k, v, pin = _split_proj(_rmsnorm(y_s, norm_mix[l]) @ w_in[l])
        k_past = cache_k[page_table, l].reshape(db, past_len, N_HEADS, HEAD_DIM)
        v_past = cache_v[page_table, l].reshape(db, past_len, N_HEADS, HEAD_DIM)
        attn = _moba_sample(q, k, v, k_past, v_past)
        pool = _pool_mix(pin, state_pool[l].astype(pin.dtype), pos_s, w_pool[l], pool_scale[l])
        y_s = y_s + jnp.concatenate([attn, pool], axis=-1) @ w_out[l]
        y_s = y_s + _peer_ffn(_rmsnorm(y_s, norm_ffn[l]), peer_wq[l], peer_keys[l], peer_u[l], peer_v[l])
        kss.append(k)
        vss.append(v)
        pss.append(jnp.concatenate([state_pool[l].astype(pin.dtype), pin], axis=1)[:, -POOL_BUF:])
    y_prompt = _rmsnorm(y_p, norm_final)
    y_sample = _rmsnorm(y_s, norm_final)
    k_prompt = jnp.stack(kp, axis=1)
    v_prompt = jnp.stack(vp, axis=1)
    pool_prompt = jnp.stack(pp, axis=0)
    k_sample = jnp.stack(kss, axis=1)
    v_sample = jnp.stack(vss, axis=1)
    pool_sample = jnp.stack(pss, axis=0)
    return (y_prompt, y_sample, k_prompt, v_prompt, pool_prompt, k_sample, v_sample, pool_sample)
```

```python
import functools
import math

import jax
import jax.numpy as jnp
from jax import lax
from jax.experimental import pallas as pl
from jax.experimental.pallas import tpu as pltpu

F32 = jnp.float32
BF16 = jnp.bfloat16

HEAD_DIM = 64
MOBA_BLOCK = 256
MOBA_TOPK = 3
POOL_WINDOWS = (2, 4, 8, 16)
POOL_PREFIX = 16
PEER_HEADS = 8
PEER_NKEYS = 128
PEER_TOPK = 16
RMS_EPS = 1e-6
NEG = -1e30
LOG2E = 1.4426950408889634

LANES = 128
VMEM_LIMIT = 56 * 1024 * 1024

_NT = (((1,), (1,)), ((), ()))


def _params(semantics):
    return pltpu.CompilerParams(dimension_semantics=semantics, vmem_limit_bytes=VMEM_LIMIT)


def _rms(x, g):
    return x * lax.rsqrt(jnp.mean(x * x, axis=-1, keepdims=True) + RMS_EPS) * g


def _split_bf16(x):
    hi = x.astype(BF16)
    lo = (x - hi.astype(F32)).astype(BF16)
    return hi, lo


def _dot_nt_3pass(a, b):
    a_hi, a_lo = _split_bf16(a)
    b_hi, b_lo = _split_bf16(b)
    d = functools.partial(lax.dot_general, dimension_numbers=_NT, preferred_element_type=F32)
    return d(a_hi, b_hi) + (d(a_hi, b_lo) + d(a_lo, b_hi))


def _norm_proj_kernel(x_ref, g_ref, w_ref, q_ref, k_ref, v_ref, p_ref):
    h = _rms(x_ref[...], g_ref[...])
    proj = jnp.dot(h.astype(BF16), w_ref[...], preferred_element_type=F32)
    w = q_ref.shape[1]
    q_ref[...] = proj[:, 0 * w:1 * w]
    k_ref[...] = proj[:, 1 * w:2 * w]
    v_ref[...] = proj[:, 2 * w:3 * w]
    p_ref[...] = proj[:, 3 * w:4 * w]


def _norm_proj(x, g, w16, tm):
    t, d = x.shape
    w = w16.shape[1] // 4
    row = pl.BlockSpec((tm, d), lambda i: (i, 0))
    out = pl.BlockSpec((tm, w), lambda i: (i, 0))
    return pl.pallas_call(
        _norm_proj_kernel,
        grid=(t // tm,),
        in_specs=[row, pl.BlockSpec((1, d), lambda i: (0, 0)), pl.BlockSpec(w16.shape, lambda i: (0, 0))],
        out_specs=[out] * 4,
        out_shape=[jax.ShapeDtypeStruct((t, w), F32)] * 4,
        compiler_params=_params(("parallel",)),
        name="norm_proj",
    )(x, g.reshape(1, d), w16)


def _block_rank_select(gate, n_cand):
    lane = lax.broadcasted_iota(jnp.int32, gate.shape, 1)
    rank = jnp.zeros(gate.shape, F32)
    for m in range(n_cand):
        gm = gate[:, m:m + 1]
        beats = jnp.where(gm > gate, 1.0, jnp.where(gm == gate, jnp.where(lane > m, 1.0, 0.0), 0.0))
        rank = rank + beats
    return jnp.where(lane < n_cand, rank, float(MOBA_TOPK)) < float(MOBA_TOPK)


def _moba_prompt_kernel(q_ref, k_ref, v_ref, o_ref):
    s_len = q_ref.shape[1]
    nb = s_len // MOBA_BLOCK
    k = k_ref[0]
    k16 = k.astype(BF16)
    v16 = v_ref[0].astype(BF16)
    lane = lax.broadcasted_iota(jnp.int32, (1, LANES), 1)
    head_lanes = [lane < HEAD_DIM, lane >= HEAD_DIM]
    means = [jnp.mean(k[n * MOBA_BLOCK:(n + 1) * MOBA_BLOCK], axis=0, keepdims=True) for n in range(nb)]
    kbar = jnp.concatenate(means + [jnp.zeros((LANES - nb, LANES), F32)], axis=0)
    row = lax.broadcasted_iota(jnp.int32, (MOBA_BLOCK, MOBA_BLOCK), 0)
    col = lax.broadcasted_iota(jnp.int32, (MOBA_BLOCK, MOBA_BLOCK), 1)
    causal = row >= col
    scale = HEAD_DIM ** -0.5
    for i in range(nb):
        q = q_ref[0, i * MOBA_BLOCK:(i + 1) * MOBA_BLOCK, :]
        n_keys = (i + 1) * MOBA_BLOCK
        outs = []
        for hh in range(2):
            qh = jnp.where(head_lanes[hh], q, 0.0)
            s = lax.dot_general((qh * scale).astype(BF16), k16[:n_keys], _NT, preferred_element_type=F32)
            pieces = []
            if i > 0:
                gate = _dot_nt_3pass(qh, kbar)
                sel = jnp.where(_block_rank_select(gate, i), 1.0, 0.0)
                pieces = [jnp.where(sel[:, n:n + 1] > 0.5, s[:, n * MOBA_BLOCK:(n + 1) * MOBA_BLOCK], NEG)
                          for n in range(i)]
            pieces.append(jnp.where(causal, s[:, i * MOBA_BLOCK:], NEG))
            s = jnp.concatenate(pieces, axis=1)
            m = jnp.max(s, axis=-1, keepdims=True)
            p = jnp.exp(s - m)
            l = jnp.sum(p, axis=-1, keepdims=True)
            o = jnp.dot(p.astype(BF16), v16[:n_keys], preferred_element_type=F32)
            outs.append(o / l)
        o_ref[0, i * MOBA_BLOCK:(i + 1) * MOBA_BLOCK, :] = jnp.where(head_lanes[0], outs[0], outs[1])


def _moba_prompt(q, k, v):
    b, s, w = q.shape
    spec = pl.BlockSpec((1, s, LANES), lambda bi, pi: (bi, 0, pi))
    return pl.pallas_call(
        _moba_prompt_kernel,
        grid=(b, w // LANES),
        in_specs=[spec] * 3,
        out_specs=spec,
        out_shape=jax.ShapeDtypeStruct((b, s, w), F32),
        compiler_params=_params(("parallel", "parallel")),
        name="moba_prompt",
    )(q, k, v)


def _moba_sample_kernel(pt_ref, q_ref, kn_ref, vn_ref, k0_ref, k1_ref, v0_ref, v1_ref, o_ref,
                        m_scr, l_scr, g_scr, o_scr):
    del pt_ref
    n = pl.program_id(1)
    n_blocks = pl.num_programs(1)
    tq, w = q_ref.shape[1], q_ref.shape[2]
    n_heads = w // HEAD_DIM
    rows = tq * n_heads
    scale = HEAD_DIM ** -0.5

    head_of_lane = lax.broadcasted_iota(jnp.int32, (n_heads, w), 1) // HEAD_DIM
    head_mask = jnp.where(head_of_lane == lax.broadcasted_iota(jnp.int32, (n_heads, w), 0), 1.0, 0.0)
    q = q_ref[0]
    qbd = jnp.concatenate([jnp.broadcast_to(q[j:j + 1], (n_heads, w)) * head_mask for j in range(tq)], axis=0)

    @pl.when(n == 0)
    def _():
        m_scr[...] = jnp.zeros_like(m_scr)
        l_scr[...] = jnp.zeros_like(l_scr)
        g_scr[...] = jnp.zeros_like(g_scr)

    kb = jnp.concatenate([k0_ref[0, 0], k1_ref[0, 0]], axis=0)
    vb = jnp.concatenate([v0_ref[0, 0], v1_ref[0, 0]], axis=0)
    s = lax.dot_general((qbd * scale).astype(BF16), kb.astype(BF16), _NT, preferred_element_type=F32)
    m_n = jnp.max(s, axis=-1, keepdims=True)
    p = jnp.exp(s - m_n)
    l_n = jnp.sum(p, axis=-1, keepdims=True)
    o_scr[n] = jnp.dot(p.astype(BF16), vb.astype(BF16), preferred_element_type=F32)
    kbar = jnp.sum(kb, axis=0, keepdims=True) / float(MOBA_BLOCK)
    g_n = jnp.sum(qbd * kbar, axis=-1, keepdims=True)
    blk_lane = lax.broadcasted_iota(jnp.int32, (rows, LANES), 1)
    here = blk_lane == n
    m_scr[...] = jnp.where(here, m_n, m_scr[...])
    l_scr[...] = jnp.where(here, l_n, l_scr[...])
    g_scr[...] = jnp.where(here, g_n, g_scr[...])

    @pl.when(n == n_blocks - 1)
    def _():
        n_past = o_scr.shape[0]
        sel = _block_rank_select(g_scr[...], n_past)
        m_all = m_scr[...]
        m_run = jnp.max(jnp.where(sel, m_all, NEG), axis=-1, keepdims=True)
        q_idx = lax.broadcasted_iota(jnp.int32, (rows, 1), 0) // n_heads
        kn = kn_ref[0]
        vn = vn_ref[0]
        s_own = []
        for j in range(tq):
            sj = jnp.sum(qbd * scale * kn[j:j + 1], axis=-1, keepdims=True)
            s_own.append(jnp.where(q_idx >= j, sj, NEG))
            m_run = jnp.maximum(m_run, s_own[j])
        wgt = jnp.where(sel, jnp.exp(m_all - m_run), 0.0)
        den = jnp.sum(wgt * l_scr[...], axis=-1, keepdims=True)
        num = jnp.zeros((rows, w), F32)
        for nn in range(n_past):
            num = num + wgt[:, nn:nn + 1] * o_scr[nn]
        for j in range(tq):
            e = jnp.where(q_idx >= j, jnp.exp(s_own[j] - m_run), 0.0)
            den = den + e
            num = num + e * vn[j:j + 1]
        acc = num / den
        out_rows = [jnp.sum(acc[j * n_heads:(j + 1) * n_heads] * head_mask, axis=0, keepdims=True)
                    for j in range(tq)]
        o_ref[0] = jnp.concatenate(out_rows, axis=0)


def _moba_sample(q, k_new, v_new, cache_k, cache_v, page_table, layer):
    db, tq, w = q.shape
    n_pool, _, page, _ = cache_k.shape
    n_pages = page_table.shape[1]
    assert MOBA_BLOCK == 2 * page and (n_pages * page) % MOBA_BLOCK == 0
    n_past = n_pages * page // MOBA_BLOCK
    assert n_past <= LANES and tq <= MOBA_BLOCK
    rows = tq * (w // HEAD_DIM)
    tok = pl.BlockSpec((1, tq, w), lambda b, n, pt: (b, 0, 0))

    def page_spec(which):
        return pl.BlockSpec((1, 1, page, w), lambda b, n, pt: (pt[b, 2 * n + which], layer, 0, 0))

    grid_spec = pltpu.PrefetchScalarGridSpec(
        num_scalar_prefetch=1,
        grid=(db, n_past),
        in_specs=[tok, tok, tok, page_spec(0), page_spec(1), page_spec(0), page_spec(1)],
        out_specs=tok,
        scratch_shapes=[pltpu.VMEM((rows, LANES), F32)] * 3 + [pltpu.VMEM((n_past, rows, w), F32)],
    )
    return pl.pallas_call(
        _moba_sample_kernel,
        grid_spec=grid_spec,
        out_shape=jax.ShapeDtypeStruct((db, tq, w), F32),
        compiler_params=_params(("parallel", "arbitrary")),
        name="moba_sample",
    )(page_table, q, k_new, v_new, cache_k, cache_k, cache_v, cache_v)


def _pool_mix_kernel(x_ref, pre_ref, w_ref, sc_ref, o_ref, xp_scr, *, pos0):
    t = x_ref.shape[1]
    gw = w_ref.shape[1]
    xp_scr[0:POOL_PREFIX, :] = pre_ref[0]
    xp_scr[POOL_PREFIX:POOL_PREFIX + t, :] = x_ref[0]
    pos = (lax.broadcasted_iota(jnp.int32, (t, 1), 0) + (pos0 + 1)).astype(F32)
    outs = []
    for g, win in enumerate(POOL_WINDOWS):
        sl = slice(g * gw, (g + 1) * gw)
        cur = xp_scr[POOL_PREFIX:POOL_PREFIX + t, sl]
        acc = cur
        for j in range(1, win):
            acc = acc + xp_scr[POOL_PREFIX - j:POOL_PREFIX - j + t, sl]
        r = acc / jnp.minimum(pos, float(win)) - cur
        outs.append(jnp.dot(r.astype(BF16), w_ref[g], preferred_element_type=F32))
    o_ref[0] = jnp.concatenate(outs, axis=1) * sc_ref[...]


def _pool_mix(pin, prefix, w_pool16, scale, pos0):
    b, t, w = pin.shape
    seq = pl.BlockSpec((1, t, w), lambda i: (i, 0, 0))
    return pl.pallas_call(
        functools.partial(_pool_mix_kernel, pos0=pos0),
        grid=(b,),
        in_specs=[seq, pl.BlockSpec((1, POOL_PREFIX, w), lambda i: (i, 0, 0)),
                  pl.BlockSpec(w_pool16.shape, lambda i: (0, 0, 0)), pl.BlockSpec((1, w), lambda i: (0, 0))],
        out_specs=seq,
        out_shape=jax.ShapeDtypeStruct((b, t, w), F32),
        scratch_shapes=[pltpu.VMEM((POOL_PREFIX + t, w), F32)],
        compiler_params=_params(("parallel",)),
        name="pool_mix",
    )(pin, prefix, w_pool16, scale.reshape(1, w))


def _mix_out_kernel(x_ref, a_ref, p_ref, wo_ref, g_ref, wq_ref, keys_ref, x1_ref, h_ref, s1_ref, s2_ref):
    aw = a_ref.shape[1]
    mix = (jnp.dot(a_ref[...].astype(BF16), wo_ref[0:aw, :], preferred_element_type=F32)
           + jnp.dot(p_ref[...].astype(BF16), wo_ref[aw:, :], preferred_element_type=F32))
    x1 = x_ref[...] + mix
    x1_ref[...] = x1
    h16 = _rms(x1, g_ref[...]).astype(BF16)
    h_ref[...] = h16
    qq = jnp.dot(h16, wq_ref[...], preferred_element_type=F32)
    half = keys_ref.shape[2]
    for hd in range(PEER_HEADS):
        q1 = qq[:, (2 * hd) * half:(2 * hd + 1) * half]
        q2 = qq[:, (2 * hd + 1) * half:(2 * hd + 2) * half]
        s1_ref[hd] = _dot_nt_3pass(keys_ref[0], q1)
        s2_ref[hd] = _dot_nt_3pass(keys_ref[1], q2)


def _mix_out(x, attn, pool, wo16, g, wq16, keys, tm):
    t, d = x.shape
    aw = attn.shape[1]
    nk = keys.shape[1]
    row = lambda wdt: pl.BlockSpec((tm, wdt), lambda i: (i, 0))
    full = lambda a: pl.BlockSpec(a.shape, lambda i: (0,) * a.ndim)
    sc = pl.BlockSpec((PEER_HEADS, nk, tm), lambda i: (0, 0, i))
    return pl.pallas_call(
        _mix_out_kernel,
        grid=(t // tm,),
        in_specs=[row(d), row(aw), row(aw), full(wo16), pl.BlockSpec((1, d), lambda i: (0, 0)), full(wq16), full(keys)],
        out_specs=[row(d), row(d), sc, sc],
        out_shape=[jax.ShapeDtypeStruct((t, d), F32), jax.ShapeDtypeStruct((t, d), BF16),
                   jax.ShapeDtypeStruct((PEER_HEADS, nk, t), F32), jax.ShapeDtypeStruct((PEER_HEADS, nk, t), F32)],
        compiler_params=_params(("parallel",)),
        name="mix_out",
    )(x, attn, pool, wo16, g.reshape(1, d), wq16, keys)


def _top_values(s, count):
    vals = []
    cur = s
    for _ in range(count):
        mx = jnp.max(cur, axis=0, keepdims=True)
        vals.append(mx)
        cur = jnp.where(cur == mx, NEG, cur)
    return vals


def _peer_gate_kernel(s1_ref, s2_ref, x1_ref, m2_ref, tau_ref):
    s1 = s1_ref[0]
    s2 = s2_ref[0]
    v1 = _top_values(s1, PEER_TOPK)
    v2 = _top_values(s2, PEER_TOPK)
    v2m = jnp.concatenate(v2, axis=0)
    cand = jnp.concatenate([a + v2m for a in v1], axis=0)
    tau = _top_values(cand, PEER_TOPK)[-1]
    cmax = v1[0] + v2[0]
    chosen = cand >= tau
    z = jnp.sum(jnp.where(chosen, jnp.exp(cand - cmax), 0.0), axis=0, keepdims=True)
    lse = cmax + jnp.log(z)
    x1 = jnp.where(s1 >= v1[-1], (s1 - lse) * LOG2E, NEG)
    m2 = jnp.where(s2 >= v2[-1], s2 * LOG2E, NEG)
    v2s = v2m * LOG2E
    cand_s = jnp.concatenate([(a - lse) * LOG2E + v2s for a in v1], axis=0)
    x1_ref[0] = x1
    m2_ref[0] = m2
    tau_ref[0] = jnp.min(jnp.where(chosen, cand_s, -NEG), axis=0, keepdims=True)


def _peer_gate(s1t, s2t, tl):
    nh, nk, t = s1t.shape
    spec = pl.BlockSpec((1, nk, tl), lambda h, i: (h, 0, i))
    tspec = pl.BlockSpec((1, 1, tl), lambda h, i: (h, 0, i))
    return pl.pallas_call(
        _peer_gate_kernel,
        grid=(nh, t // tl),
        in_specs=[spec, spec],
        out_specs=[spec, spec, tspec],
        out_shape=[jax.ShapeDtypeStruct((nh, nk, t), F32)] * 2 + [jax.ShapeDtypeStruct((nh, 1, t), F32)],
        compiler_params=_params(("parallel", "parallel")),
        name="peer_gate",
    )(s1t, s2t)


def _peer_dense_kernel(h_ref, x_ref, u_ref, vt_ref, x1_ref, m2_ref, tau_ref, y_ref, a_scr, w_scr, acc_scr):
    j = pl.program_id(1)
    tt = h_ref.shape[0]
    nk = m2_ref.shape[1]
    n_i1 = x1_ref.shape[1]

    @pl.when(j == 0)
    def _():
        acc_scr[...] = jnp.zeros_like(acc_scr)

    a_scr[...] = lax.dot_general(u_ref[...], h_ref[...], _NT, preferred_element_type=F32)

    def per_lane_tile(lt, carry):
        ls = pl.ds(pl.multiple_of(lt * LANES, LANES), LANES)
        for c in range(n_i1):
            rows = slice(c * nk, (c + 1) * nk)
            gate = jnp.zeros((nk, LANES), F32)
            for hd in range(PEER_HEADS):
                sc = x1_ref[hd, c:c + 1, ls] + m2_ref[hd, :, ls]
                gate = gate + jnp.where(sc >= tau_ref[hd, :, ls], jnp.exp2(sc), 0.0)
            a = a_scr[rows, ls]
            gelu = 0.5 * a * (1.0 + lax.erf(a * (2.0 ** -0.5)))
            w_scr[rows, ls] = (gate * gelu).astype(BF16)
        return carry

    lax.fori_loop(0, tt // LANES, per_lane_tile, 0)
    acc_scr[...] += jnp.dot(vt_ref[...], w_scr[...], preferred_element_type=F32)

    @pl.when(j == pl.num_programs(1) - 1)
    def _():
        y_ref[...] = x_ref[...] + acc_scr[...].T


def _peer_dense(h16, x, u16, vt16, x1t, m2t, tau, tt, eb):
    t, d = x.shape
    e = u16.shape[0]
    nh, nk, _ = x1t.shape
    tok = lambda: pl.BlockSpec((tt, d), lambda i, j: (i, 0))
    return pl.pallas_call(
        _peer_dense_kernel,
        grid=(t // tt, e // eb),
        in_specs=[tok(), tok(),
                  pl.BlockSpec((eb, d), lambda i, j: (j, 0)),
                  pl.BlockSpec((d, eb), lambda i, j: (0, j)),
                  pl.BlockSpec((nh, eb // nk, tt), lambda i, j: (0, j, i)),
                  pl.BlockSpec((nh, nk, tt), lambda i, j: (0, 0, i)),
                  pl.BlockSpec((nh, 1, tt), lambda i, j: (0, 0, i))],
        out_specs=tok(),
        out_shape=jax.ShapeDtypeStruct((t, d), F32),
        scratch_shapes=[pltpu.VMEM((eb, tt), F32), pltpu.VMEM((eb, tt), BF16), pltpu.VMEM((d, tt), F32)],
        compiler_params=_params(("parallel", "arbitrary")),
        name="peer_dense",
    )(h16, x, u16, vt16, x1t, m2t, tau)


def _final_norm_kernel(x_ref, g_ref, o_ref):
    o_ref[...] = _rms(x_ref[...], g_ref[...])


def _final_norm(x, g, tm):
    t, d = x.shape
    row = pl.BlockSpec((tm, d), lambda i: (i, 0))
    return pl.pallas_call(
        _final_norm_kernel,
        grid=(t // tm,),
        in_specs=[row, pl.BlockSpec((1, d), lambda i: (0, 0))],
        out_specs=row,
        out_shape=jax.ShapeDtypeStruct((t, d), F32),
        compiler_params=_params(("parallel",)),
        name="final_norm",
    )(x, g.reshape(1, d))


def _token_tile(t, want):
    return want if t % want == 0 else t


def _ffn(x, attn, pool, wo16, g_ffn, wq16, keys, u16, vt16):
    t = x.shape[0]
    x1, h16, s1t, s2t = _mix_out(x, attn, pool, wo16, g_ffn, wq16, keys, _token_tile(t, 256))
    x1t, m2t, tau = _peer_gate(s1t, s2t, _token_tile(t, 512))
    return _peer_dense(h16, x1, u16, vt16, x1t, m2t, tau, _token_tile(t, 512), 1024)


def kernel(x_prompt, x_sample, cache_k, cache_v, state_pool, page_table, norm_mix, w_in, w_pool, pool_scale,
           w_out, norm_ffn, peer_wq, peer_keys, peer_u, peer_v, norm_final):
    bp, sp, d = x_prompt.shape
    db, ts, _ = x_sample.shape
    depth = w_in.shape[0]
    aw = w_in.shape[2] // 4
    n_heads = aw // HEAD_DIM
    pool_buf = state_pool.shape[2]
    assert pool_buf == POOL_PREFIX - 1 and sp % MOBA_BLOCK == 0 and aw % LANES == 0
    n_pool, _, page, _, _ = cache_k.shape
    ck = cache_k.reshape(n_pool, depth, page, aw)
    cv = cache_v.reshape(n_pool, depth, page, aw)
    past_len = page_table.shape[1] * page
    ts_pad = -(-ts // 8) * 8

    y_p = x_prompt.reshape(bp * sp, d)
    y_s = x_sample.reshape(db * ts, d)
    kp, vp, pp, kss, vss, pss = [], [], [], [], [], []
    for l in range(depth):
        w_in16 = w_in[l].astype(BF16)
        w_pool16 = w_pool[l].astype(BF16)
        wo16 = w_out[l].astype(BF16)
        wq16 = peer_wq[l].astype(BF16)
        u16 = peer_u[l].astype(BF16)
        vt16 = peer_v[l].T.astype(BF16)
        q, k, v, pin = _norm_proj(y_p, norm_mix[l], w_in16, 512)
        q3, k3, v3, pin3 = (a.reshape(bp, sp, aw) for a in (q, k, v, pin))
        attn = _moba_prompt(q3, k3, v3)
        pool = _pool_mix(pin3, jnp.zeros((bp, POOL_PREFIX, aw), F32), w_pool16, pool_scale[l], 0)
        y_p = _ffn(y_p, attn.reshape(bp * sp, aw), pool.reshape(bp * sp, aw), wo16, norm_ffn[l], wq16,
                   peer_keys[l], u16, vt16)
        kp.append(k3.reshape(bp, sp, n_heads, HEAD_DIM))
        vp.append(v3.reshape(bp, sp, n_heads, HEAD_DIM))
        pp.append(pin3[:, sp - pool_buf:])
        q, k, v, pin = _norm_proj(y_s, norm_mix[l], w_in16, db * ts)
        q3, k3, v3, pin3 = (a.reshape(db, ts, aw) for a in (q, k, v, pin))
        attn = _moba_sample(q3, k3, v3, ck, cv, page_table, l)
        prefix = jnp.concatenate([jnp.zeros((db, 1, aw), F32), state_pool[l]], axis=1)
        pin_pad = jnp.pad(pin3, ((0, 0), (0, ts_pad - ts), (0, 0)))
        pool = _pool_mix(pin_pad, prefix, w_pool16, pool_scale[l], past_len)[:, :ts]
        y_s = _ffn(y_s, attn.reshape(db * ts, aw), pool.reshape(db * ts, aw), wo16, norm_ffn[l], wq16,
                   peer_keys[l], u16, vt16)
        kss.append(k3.reshape(db, ts, n_heads, HEAD_DIM))
        vss.append(v3.reshape(db, ts, n_heads, HEAD_DIM))
        pss.append(jnp.concatenate([state_pool[l], pin3], axis=1)[:, ts:])
    y_prompt = _final_norm(y_p, norm_final, 512).reshape(bp, sp, d)
    y_sample = _final_norm(y_s, norm_final, db * ts).reshape(db, ts, d)
    return (y_prompt, y_sample, jnp.stack(kp, axis=1), jnp.stack(vp, axis=1), jnp.stack(pp, axis=0),
            jnp.stack(kss, axis=1), jnp.stack(vss, axis=1), jnp.stack(pss, axis=0))
```

```python
import functools
import math

import jax
import jax.numpy as jnp
from jax import lax
from jax.experimental import pallas as pl
from jax.experimental.pallas import tpu as pltpu

F32 = jnp.float32
BF16 = jnp.bfloat16

HEAD_DIM = 64
MOBA_BLOCK = 256
MOBA_TOPK = 3
POOL_WINDOWS = (2, 4, 8, 16)
POOL_PREFIX = 16
PEER_HEADS = 8
PEER_NKEYS = 128
PEER_TOPK = 16
RMS_EPS = 1e-6
NEG = -1e30
LOG2E = 1.4426950408889634

LANES = 128
VMEM_LIMIT = 56 * 1024 * 1024

_NT = (((1,), (1,)), ((), ()))


def _params(semantics):
    return pltpu.CompilerParams(dimension_semantics=semantics, vmem_limit_bytes=VMEM_LIMIT)


def _rms(x, g):
    return x * lax.rsqrt(jnp.mean(x * x, axis=-1, keepdims=True) + RMS_EPS) * g


def _split_bf16(x):
    hi = x.astype(BF16)
    lo = (x - hi.astype(F32)).astype(BF16)
    return hi, lo


def _dot_nt_3pass(a, b):
    a_hi, a_lo = _split_bf16(a)
    b_hi, b_lo = _split_bf16(b)
    d = functools.partial(lax.dot_general, dimension_numbers=_NT, preferred_element_type=F32)
    return d(a_hi, b_hi) + (d(a_hi, b_lo) + d(a_lo, b_hi))


def _norm_proj_kernel(x_ref, g_ref, w_ref, q_ref, k_ref, v_ref, p_ref):
    h = _rms(x_ref[...], g_ref[...])
    proj = jnp.dot(h.astype(BF16), w_ref[...], preferred_element_type=F32)
    w = q_ref.shape[1]
    q_ref[...] = proj[:, 0 * w:1 * w]
    k_ref[...] = proj[:, 1 * w:2 * w]
    v_ref[...] = proj[:, 2 * w:3 * w]
    p_ref[...] = proj[:, 3 * w:4 * w]


def _norm_proj(x, g, w16, tm):
    t, d = x.shape
    w = w16.shape[1] // 4
    row = pl.BlockSpec((tm, d), lambda i: (i, 0))
    out = pl.BlockSpec((tm, w), lambda i: (i, 0))
    return pl.pallas_call(
        _norm_proj_kernel,
        grid=(t // tm,),
        in_specs=[row, pl.BlockSpec((1, d), lambda i: (0, 0)), pl.BlockSpec(w16.shape, lambda i: (0, 0))],
        out_specs=[out] * 4,
        out_shape=[jax.ShapeDtypeStruct((t, w), F32)] * 4,
        compiler_params=_params(("parallel",)),
        name="norm_proj",
    )(x, g.reshape(1, d), w16)


def _block_rank_select(gate, n_cand):
    lane = lax.broadcasted_iota(jnp.int32, gate.shape, 1)
    rank = jnp.zeros(gate.shape, F32)
    for m in range(n_cand):
        gm = gate[:, m:m + 1]
        beats = jnp.where(gm > gate, 1.0, jnp.where(gm == gate, jnp.where(lane > m, 1.0, 0.0), 0.0))
        rank = rank + beats
    return jnp.where(lane < n_cand, rank, float(MOBA_TOPK)) < float(MOBA_TOPK)


def _moba_prompt_kernel(q_ref, k_ref, v_ref, o_ref):
    s_len = q_ref.shape[1]
    nb = s_len // MOBA_BLOCK
    k = k_ref[0]
    k16 = k.astype(BF16)
    v16 = v_ref[0].astype(BF16)
    lane = lax.broadcasted_iota(jnp.int32, (1, LANES), 1)
    head_lanes = [lane < HEAD_DIM, lane >= HEAD_DIM]
    means = [jnp.mean(k[n * MOBA_BLOCK:(n + 1) * MOBA_BLOCK], axis=0, keepdims=True) for n in range(nb)]
    kbar = jnp.concatenate(means + [jnp.zeros((LANES - nb, LANES), F32)], axis=0)
    row = lax.broadcasted_iota(jnp.int32, (MOBA_BLOCK, MOBA_BLOCK), 0)
    col = lax.broadcasted_iota(jnp.int32, (MOBA_BLOCK, MOBA_BLOCK), 1)
    causal = row >= col
    scale = HEAD_DIM ** -0.5
    for i in range(nb):
        q = q_ref[0, i * MOBA_BLOCK:(i + 1) * MOBA_BLOCK, :]
        n_keys = (i + 1) * MOBA_BLOCK
        outs = []
        for hh in range(2):
            qh = jnp.where(head_lanes[hh], q, 0.0)
            s = lax.dot_general((qh * scale).astype(BF16), k16[:n_keys], _NT, preferred_element_type=F32)
            pieces = []
            if i > 0:
                gate = _dot_nt_3pass(qh, kbar)
                sel = jnp.where(_block_rank_select(gate, i), 1.0, 0.0)
                pieces = [jnp.where(sel[:, n:n + 1] > 0.5, s[:, n * MOBA_BLOCK:(n + 1) * MOBA_BLOCK], NEG)
                          for n in range(i)]
            pieces.append(jnp.where(causal, s[:, i * MOBA_BLOCK:], NEG))
            s = jnp.concatenate(pieces, axis=1)
            m = jnp.max(s, axis=-1, keepdims=True)
            p = jnp.exp(s - m)
            l = jnp.sum(p, axis=-1, keepdims=True)
            o = jnp.dot(p.astype(BF16), v16[:n_keys], preferred_element_type=F32)
            outs.append(o / l)
        o_ref[0, i * MOBA_BLOCK:(i + 1) * MOBA_BLOCK, :] = jnp.where(head_lanes[0], outs[0], outs[1])


def _moba_prompt(q, k, v):
    b, s, w = q.shape
    spec = pl.BlockSpec((1, s, LANES), lambda bi, pi: (bi, 0, pi))
    return pl.pallas_call(
        _moba_prompt_kernel,
        grid=(b, w // LANES),
        in_specs=[spec] * 3,
        out_specs=spec,
        out_shape=jax.ShapeDtypeStruct((b, s, w), F32),
        compiler_params=_params(("parallel", "parallel")),
        name="moba_prompt",
    )(q, k, v)


def _moba_sample_kernel(pt_ref, q_ref, kn_ref, vn_ref, *refs, blocks_per_step, pages_per_block):
    del pt_ref
    n_pg = blocks_per_step * pages_per_block
    kt_refs, vt_refs = refs[:n_pg], refs[n_pg:2 * n_pg]
    o_ref, m_scr, l_scr, g_scr, o_scr = refs[2 * n_pg:]
    step = pl.program_id(1)
    tq, w = q_ref.shape[1], q_ref.shape[2]
    n_heads = w // HEAD_DIM
    rows = tq * n_heads
    scale = HEAD_DIM ** -0.5

    head_of_lane = lax.broadcasted_iota(jnp.int32, (n_heads, w), 1) // HEAD_DIM
    head_mask = jnp.where(head_of_lane == lax.broadcasted_iota(jnp.int32, (n_heads, w), 0), 1.0, 0.0)
    q = q_ref[0]
    qbd = jnp.concatenate([jnp.broadcast_to(q[j:j + 1], (n_heads, w)) * head_mask for j in range(tq)], axis=0)

    @pl.when(step == 0)
    def _():
        m_scr[...] = jnp.zeros_like(m_scr)
        l_scr[...] = jnp.zeros_like(l_scr)
        g_scr[...] = jnp.zeros_like(g_scr)

    qs16 = (qbd * scale).astype(BF16)
    blk_lane = lax.broadcasted_iota(jnp.int32, (rows, LANES), 1)
    m_all, l_all, g_all = m_scr[...], l_scr[...], g_scr[...]
    for jj in range(blocks_per_step):
        n = step * blocks_per_step + jj
        pgs = range(jj * pages_per_block, (jj + 1) * pages_per_block)
        s = jnp.concatenate([jnp.dot(qs16, kt_refs[pg][0, 0].astype(BF16), preferred_element_type=F32)
                             for pg in pgs], axis=1)
        m_n = jnp.max(s, axis=-1, keepdims=True)
        p = jnp.exp(s - m_n)
        l_n = jnp.sum(p, axis=-1, keepdims=True)
        g_n = jnp.sum(s, axis=-1, keepdims=True)
        p16 = p.astype(BF16)
        page = kt_refs[0].shape[3]
        o_n = None
        for i, pg in enumerate(pgs):
            part = lax.dot_general(p16[:, i * page:(i + 1) * page], vt_refs[pg][0, 0].astype(BF16), _NT,
                                   preferred_element_type=F32)
            o_n = part if o_n is None else o_n + part
        o_scr[n] = o_n
        here = blk_lane == n
        m_all = jnp.where(here, m_n, m_all)
        l_all = jnp.where(here, l_n, l_all)
        g_all = jnp.where(here, g_n, g_all)
    m_scr[...] = m_all
    l_scr[...] = l_all
    g_scr[...] = g_all

    @pl.when(step == pl.num_programs(1) - 1)
    def _():
        n_past = o_scr.shape[0]
        sel = _block_rank_select(g_scr[...], n_past)
        m_all = m_scr[...]
        m_run = jnp.max(jnp.where(sel, m_all, NEG), axis=-1, keepdims=True)
        q_idx = lax.broadcasted_iota(jnp.int32, (rows, 1), 0) // n_heads
        kn = kn_ref[0]
        vn = vn_ref[0]
        s_own = []
        for j in range(tq):
            sj = jnp.sum(qbd * scale * kn[j:j + 1], axis=-1, keepdims=True)
            s_own.append(jnp.where(q_idx >= j, sj, NEG))
            m_run = jnp.maximum(m_run, s_own[j])
        wgt = jnp.where(sel, jnp.exp(m_all - m_run), 0.0)
        den = jnp.sum(wgt * l_scr[...], axis=-1, keepdims=True)
        num = jnp.zeros((rows, w), F32)
        for nn in range(n_past):
            num = num + wgt[:, nn:nn + 1] * o_scr[nn]
        for j in range(tq):
            e = jnp.where(q_idx >= j, jnp.exp(s_own[j] - m_run), 0.0)
            den = den + e
            num = num + e * vn[j:j + 1]
        acc = num / den
        out_rows = [jnp.sum(acc[j * n_heads:(j + 1) * n_heads] * head_mask, axis=0, keepdims=True)
                    for j in range(tq)]
        o_ref[0] = jnp.concatenate(out_rows, axis=0)


SAMPLE_BLOCKS_PER_STEP = 4


def _moba_sample(q, k_new, v_new, cache_kt, cache_vt, page_table, layer):
    db, tq, w = q.shape
    page = cache_kt.shape[3]
    n_pages = page_table.shape[1]
    ppb = MOBA_BLOCK // page
    assert MOBA_BLOCK == ppb * page and n_pages % (ppb * SAMPLE_BLOCKS_PER_STEP) == 0
    n_past = n_pages // ppb
    assert n_past <= LANES and tq <= MOBA_BLOCK
    rows = tq * (w // HEAD_DIM)
    n_pg = SAMPLE_BLOCKS_PER_STEP * ppb
    tok = pl.BlockSpec((1, tq, w), lambda b, s, pt: (b, 0, 0))

    def page_spec(which):
        return pl.BlockSpec((1, 1, w, page), lambda b, s, pt: (pt[b, s * n_pg + which], layer, 0, 0))

    pages = [page_spec(i) for i in range(n_pg)]
    grid_spec = pltpu.PrefetchScalarGridSpec(
        num_scalar_prefetch=1,
        grid=(db, n_past // SAMPLE_BLOCKS_PER_STEP),
        in_specs=[tok, tok, tok] + pages + pages,
        out_specs=tok,
        scratch_shapes=[pltpu.VMEM((rows, LANES), F32)] * 3 + [pltpu.VMEM((n_past, rows, w), F32)],
    )
    return pl.pallas_call(
        functools.partial(_moba_sample_kernel, blocks_per_step=SAMPLE_BLOCKS_PER_STEP, pages_per_block=ppb),
        grid_spec=grid_spec,
        out_shape=jax.ShapeDtypeStruct((db, tq, w), F32),
        compiler_params=_params(("parallel", "arbitrary")),
        name="moba_sample",
    )(page_table, q, k_new, v_new, *([cache_kt] * n_pg), *([cache_vt] * n_pg))


def _pool_mix_kernel(x_ref, pre_ref, w_ref, sc_ref, o_ref, xp_scr, *, pos0):
    t = x_ref.shape[1]
    gw = w_ref.shape[1]
    xp_scr[0:POOL_PREFIX, :] = pre_ref[0]
    xp_scr[POOL_PREFIX:POOL_PREFIX + t, :] = x_ref[0]
    pos = (lax.broadcasted_iota(jnp.int32, (t, 1), 0) + (pos0 + 1)).astype(F32)
    outs = []
    for g, win in enumerate(POOL_WINDOWS):
        sl = slice(g * gw, (g + 1) * gw)
        cur = xp_scr[POOL_PREFIX:POOL_PREFIX + t, sl]
        acc = cur
        for j in range(1, win):
            acc = acc + xp_scr[POOL_PREFIX - j:POOL_PREFIX - j + t, sl]
        r = acc / jnp.minimum(pos, float(win)) - cur
        outs.append(jnp.dot(r.astype(BF16), w_ref[g], preferred_element_type=F32))
    o_ref[0] = jnp.concatenate(outs, axis=1) * sc_ref[...]


def _pool_mix(pin, prefix, w_pool16, scale, pos0):
    b, t, w = pin.shape
    seq = pl.BlockSpec((1, t, w), lambda i: (i, 0, 0))
    return pl.pallas_call(
        functools.partial(_pool_mix_kernel, pos0=pos0),
        grid=(b,),
        in_specs=[seq, pl.BlockSpec((1, POOL_PREFIX, w), lambda i: (i, 0, 0)),
                  pl.BlockSpec(w_pool16.shape, lambda i: (0, 0, 0)), pl.BlockSpec((1, w), lambda i: (0, 0))],
        out_specs=seq,
        out_shape=jax.ShapeDtypeStruct((b, t, w), F32),
        scratch_shapes=[pltpu.VMEM((POOL_PREFIX + t, w), F32)],
        compiler_params=_params(("parallel",)),
        name="pool_mix",
    )(pin, prefix, w_pool16, scale.reshape(1, w))


def _mix_out_kernel(x_ref, a_ref, p_ref, wo_ref, g_ref, wq_ref, keys_ref, x1_ref, h_ref, s1_ref, s2_ref):
    aw = a_ref.shape[1]
    mix = (jnp.dot(a_ref[...].astype(BF16), wo_ref[0:aw, :], preferred_element_type=F32)
           + jnp.dot(p_ref[...].astype(BF16), wo_ref[aw:, :], preferred_element_type=F32))
    x1 = x_ref[...] + mix
    x1_ref[...] = x1
    h16 = _rms(x1, g_ref[...]).astype(BF16)
    h_ref[...] = h16
    qq = jnp.dot(h16, wq_ref[...], preferred_element_type=F32)
    half = keys_ref.shape[2]
    for hd in range(PEER_HEADS):
        q1 = qq[:, (2 * hd) * half:(2 * hd + 1) * half]
        q2 = qq[:, (2 * hd + 1) * half:(2 * hd + 2) * half]
        s1_ref[hd] = _dot_nt_3pass(keys_ref[0], q1)
        s2_ref[hd] = _dot_nt_3pass(keys_ref[1], q2)


def _mix_out(x, attn, pool, wo16, g, wq16, keys, tm):
    t, d = x.shape
    aw = attn.shape[1]
    nk = keys.shape[1]
    row = lambda wdt: pl.BlockSpec((tm, wdt), lambda i: (i, 0))
    full = lambda a: pl.BlockSpec(a.shape, lambda i: (0,) * a.ndim)
    sc = pl.BlockSpec((PEER_HEADS, nk, tm), lambda i: (0, 0, i))
    return pl.pallas_call(
        _mix_out_kernel,
        grid=(t // tm,),
        in_specs=[row(d), row(aw), row(aw), full(wo16), pl.BlockSpec((1, d), lambda i: (0, 0)), full(wq16), full(keys)],
        out_specs=[row(d), row(d), sc, sc],
        out_shape=[jax.ShapeDtypeStruct((t, d), F32), jax.ShapeDtypeStruct((t, d), BF16),
                   jax.ShapeDtypeStruct((PEER_HEADS, nk, t), F32), jax.ShapeDtypeStruct((PEER_HEADS, nk, t), F32)],
        compiler_params=_params(("parallel",)),
        name="mix_out",
    )(x, attn, pool, wo16, g.reshape(1, d), wq16, keys)


def _top_values(s, count):
    vals = []
    cur = s
    for _ in range(count):
        mx = jnp.max(cur, axis=0, keepdims=True)
        vals.append(mx)
        cur = jnp.where(cur == mx, NEG, cur)
    return vals


def _peer_gate_kernel(s1_ref, s2_ref, e1_ref, e2_ref, thr_ref):
    s1 = s1_ref[0]
    s2 = s2_ref[0]
    v1 = _top_values(s1, PEER_TOPK)
    v2 = _top_values(s2, PEER_TOPK)
    pairs = [(a, b) for a in range(PEER_TOPK) for b in range(PEER_TOPK // (a + 1))]
    pad = [jnp.full_like(v1[0], NEG)] * (-len(pairs) % 8)
    cand = jnp.concatenate([v1[a] + v2[b] for a, b in pairs] + pad, axis=0)
    tau = _top_values(cand, PEER_TOPK)[-1]
    cmax = v1[0] + v2[0]
    chosen = cand >= tau
    z = jnp.sum(jnp.where(chosen, jnp.exp(cand - cmax), 0.0), axis=0, keepdims=True)
    e1 = lambda s: jnp.exp(s - v1[0]) / z
    e2 = lambda s: jnp.exp(s - v2[0])
    e1_ref[0] = jnp.where(s1 >= v1[-1], e1(s1), 0.0)
    e2_ref[0] = jnp.where(s2 >= v2[-1], e2(s2), 0.0)
    e1v = [e1(v) for v in v1]
    e2v = [e2(v) for v in v2]
    prod = jnp.concatenate([e1v[a] * e2v[b] for a, b in pairs] + [jnp.zeros_like(p) for p in pad], axis=0)
    thr_ref[0] = jnp.min(jnp.where(chosen, prod, 1.0), axis=0, keepdims=True)


def _peer_gate(s1t, s2t, tl):
    nh, nk, t = s1t.shape
    spec = pl.BlockSpec((1, nk, tl), lambda h, i: (h, 0, i))
    tspec = pl.BlockSpec((1, 1, tl), lambda h, i: (h, 0, i))
    return pl.pallas_call(
        _peer_gate_kernel,
        grid=(nh, t // tl),
        in_specs=[spec, spec],
        out_specs=[spec, spec, tspec],
        out_shape=[jax.ShapeDtypeStruct((nh, nk, t), F32)] * 2 + [jax.ShapeDtypeStruct((nh, 1, t), F32)],
        compiler_params=_params(("parallel", "parallel")),
        name="peer_gate",
    )(s1t, s2t)


PEER_UNIT = 256
PEER_SUB_BLOCK = 512


def _peer_dense_kernel(h_ref, x_ref, u_ref, vt_ref, e1_ref, e2_ref, thr_ref, y_ref, a_scr, w_scr, acc_scr):
    j = pl.program_id(1)
    tt = h_ref.shape[0]
    nk = e2_ref.shape[1]
    eb = u_ref.shape[0]

    @pl.when(j == 0)
    def _():
        acc_scr[...] = jnp.zeros_like(acc_scr)

    h = h_ref[...]
    n_units = eb // PEER_UNIT
    units_per_sub = PEER_SUB_BLOCK // PEER_UNIT

    def pre_activations(u):
        sl = slice(u * PEER_UNIT, (u + 1) * PEER_UNIT)
        a_scr[sl, :] = lax.dot_general(u_ref[sl, :], h, _NT, preferred_element_type=F32)

    def gate_unit(u):
        for c in range(u * PEER_UNIT // nk, (u + 1) * PEER_UNIT // nk):
            rows = slice(c * nk, (c + 1) * nk)
            for lt in range(tt // LANES):
                ls = slice(lt * LANES, (lt + 1) * LANES)
                gate = None
                for hd in range(PEER_HEADS):
                    p = e1_ref[hd, c:c + 1, ls] * e2_ref[hd, :, ls]
                    g = jnp.where(p >= thr_ref[hd, :, ls], p, 0.0)
                    gate = g if gate is None else gate + g
                a = a_scr[rows, ls]
                gelu = 0.5 * a * (1.0 + lax.erf(a * (2.0 ** -0.5)))
                w_scr[rows, ls] = (gate * gelu).astype(BF16)

    pre_activations(0)
    for u in range(n_units):
        if u + 1 < n_units:
            pre_activations(u + 1)
        gate_unit(u)
        if (u + 1) % units_per_sub == 0:
            sub = slice((u + 1 - units_per_sub) * PEER_UNIT, (u + 1) * PEER_UNIT)
            acc_scr[...] += jnp.dot(vt_ref[:, sub], w_scr[sub, :], preferred_element_type=F32)

    @pl.when(j == pl.num_programs(1) - 1)
    def _():
        y_ref[...] = x_ref[...] + acc_scr[...].T


def _peer_dense(h16, x, u16, vt16, x1t, m2t, tau, tt, eb):
    t, d = x.shape
    e = u16.shape[0]
    nh, nk, _ = x1t.shape
    assert eb % PEER_SUB_BLOCK == 0 and PEER_SUB_BLOCK % PEER_UNIT == 0 and PEER_UNIT % nk == 0
    tok = lambda: pl.BlockSpec((tt, d), lambda i, j: (i, 0))
    return pl.pallas_call(
        _peer_dense_kernel,
        grid=(t // tt, e // eb),
        in_specs=[tok(), tok(),
                  pl.BlockSpec((eb, d), lambda i, j: (j, 0)),
                  pl.BlockSpec((d, eb), lambda i, j: (0, j)),
                  pl.BlockSpec((nh, eb // nk, tt), lambda i, j: (0, j, i)),
                  pl.BlockSpec((nh, nk, tt), lambda i, j: (0, 0, i)),
                  pl.BlockSpec((nh, 1, tt), lambda i, j: (0, 0, i))],
        out_specs=tok(),
        out_shape=jax.ShapeDtypeStruct((t, d), F32),
        scratch_shapes=[pltpu.VMEM((eb, tt), F32), pltpu.VMEM((eb, tt), BF16), pltpu.VMEM((d, tt), F32)],
        compiler_params=_params(("parallel", "arbitrary")),
        name="peer_dense",
    )(h16, x, u16, vt16, x1t, m2t, tau)


def _final_norm_kernel(x_ref, g_ref, o_ref):
    o_ref[...] = _rms(x_ref[...], g_ref[...])


def _final_norm(x, g, tm):
    t, d = x.shape
    row = pl.BlockSpec((tm, d), lambda i: (i, 0))
    return pl.pallas_call(
        _final_norm_kernel,
        grid=(t // tm,),
        in_specs=[row, pl.BlockSpec((1, d), lambda i: (0, 0))],
        out_specs=row,
        out_shape=jax.ShapeDtypeStruct((t, d), F32),
        compiler_params=_params(("parallel",)),
        name="final_norm",
    )(x, g.reshape(1, d))


def _token_tile(t, want):
    return want if t % want == 0 else t


def _ffn(x, attn, pool, wo16, g_ffn, wq16, keys, u16, vt16):
    t = x.shape[0]
    x1, h16, s1t, s2t = _mix_out(x, attn, pool, wo16, g_ffn, wq16, keys, _token_tile(t, 256))
    x1t, m2t, tau = _peer_gate(s1t, s2t, _token_tile(t, 512))
    return _peer_dense(h16, x1, u16, vt16, x1t, m2t, tau, _token_tile(t, 512), 1024)


def kernel(x_prompt, x_sample, cache_k, cache_v, state_pool, page_table, norm_mix, w_in, w_pool, pool_scale,
           w_out, norm_ffn, peer_wq, peer_keys, peer_u, peer_v, norm_final):
    bp, sp, d = x_prompt.shape
    db, ts, _ = x_sample.shape
    depth = w_in.shape[0]
    aw = w_in.shape[2] // 4
    n_heads = aw // HEAD_DIM
    pool_buf = state_pool.shape[2]
    assert pool_buf == POOL_PREFIX - 1 and sp % MOBA_BLOCK == 0 and aw % LANES == 0
    n_pool, _, page, _, _ = cache_k.shape
    ck = cache_k.transpose(0, 1, 3, 4, 2).reshape(n_pool, depth, aw, page)
    cv = cache_v.transpose(0, 1, 3, 4, 2).reshape(n_pool, depth, aw, page)
    past_len = page_table.shape[1] * page
    ts_pad = -(-ts // 8) * 8

    y_p = x_prompt.reshape(bp * sp, d)
    y_s = x_sample.reshape(db * ts, d)
    kp, vp, pp, kss, vss, pss = [], [], [], [], [], []
    for l in range(depth):
        w_in16 = w_in[l].astype(BF16)
        w_pool16 = w_pool[l].astype(BF16)
        wo16 = w_out[l].astype(BF16)
        wq16 = peer_wq[l].astype(BF16)
        u16 = peer_u[l].astype(BF16)
        vt16 = peer_v[l].T.astype(BF16)
        q, k, v, pin = _norm_proj(y_p, norm_mix[l], w_in16, 512)
        q3, k3, v3, pin3 = (a.reshape(bp, sp, aw) for a in (q, k, v, pin))
        attn = _moba_prompt(q3, k3, v3)
        pool = _pool_mix(pin3, jnp.zeros((bp, POOL_PREFIX, aw), F32), w_pool16, pool_scale[l], 0)
        y_p = _ffn(y_p, attn.reshape(bp * sp, aw), pool.reshape(bp * sp, aw), wo16, norm_ffn[l], wq16,
                   peer_keys[l], u16, vt16)
        kp.append(k3.reshape(bp, sp, n_heads, HEAD_DIM))
        vp.append(v3.reshape(bp, sp, n_heads, HEAD_DIM))
        pp.append(pin3[:, sp - pool_buf:])
        q, k, v, pin = _norm_proj(y_s, norm_mix[l], w_in16, db * ts)
        q3, k3, v3, pin3 = (a.reshape(db, ts, aw) for a in (q, k, v, pin))
        attn = _moba_sample(q3, k3, v3, ck, cv, page_table, l)
        prefix = jnp.concatenate([jnp.zeros((db, 1, aw), F32), state_pool[l]], axis=1)
        pin_pad = jnp.pad(pin3, ((0, 0), (0, ts_pad - ts), (0, 0)))
        pool = _pool_mix(pin_pad, prefix, w_pool16, pool_scale[l], past_len)[:, :ts]
        y_s = _ffn(y_s, attn.reshape(db * ts, aw), pool.reshape(db * ts, aw), wo16, norm_ffn[l], wq16,
                   peer_keys[l], u16, vt16)
        kss.append(k3.reshape(db, ts, n_heads, HEAD_DIM))
        vss.append(v3.reshape(db, ts, n_heads, HEAD_DIM))
        pss.append(jnp.concatenate([state_pool[l], pin3], axis=1)[:, ts:])
    y_prompt = _final_norm(y_p, norm_final, 512).reshape(bp, sp, d)
    y_sample = _final_norm(y_s, norm_final, db * ts).reshape(db, ts, d)
    return (y_prompt, y_sample, jnp.stack(kp, axis=1), jnp.stack(vp, axis=1), jnp.stack(pp, axis=0),
            jnp.stack(kss, axis=1), jnp.stack(vss, axis=1), jnp.stack(pss, axis=0))
```

```python
import functools
import math

import jax
import jax.numpy as jnp
from jax import lax
from jax.experimental import pallas as pl
from jax.experimental.pallas import tpu as pltpu

F32 = jnp.float32
BF16 = jnp.bfloat16

HEAD_DIM = 64
MOBA_BLOCK = 256
MOBA_TOPK = 3
POOL_WINDOWS = (2, 4, 8, 16)
POOL_PREFIX = 16
PEER_HEADS = 8
PEER_NKEYS = 128
PEER_TOPK = 16
RMS_EPS = 1e-6
NEG = -1e30
LOG2E = 1.4426950408889634

LANES = 128
VMEM_LIMIT = 56 * 1024 * 1024

_NT = (((1,), (1,)), ((), ()))


def _params(semantics):
    return pltpu.CompilerParams(dimension_semantics=semantics, vmem_limit_bytes=VMEM_LIMIT)


def _rms(x, g):
    return x * lax.rsqrt(jnp.mean(x * x, axis=-1, keepdims=True) + RMS_EPS) * g


def _split_bf16(x):
    hi = x.astype(BF16)
    lo = (x - hi.astype(F32)).astype(BF16)
    return hi, lo


def _dot_nt_3pass(a, b):
    a_hi, a_lo = _split_bf16(a)
    b_hi, b_lo = _split_bf16(b)
    d = functools.partial(lax.dot_general, dimension_numbers=_NT, preferred_element_type=F32)
    return d(a_hi, b_hi) + (d(a_hi, b_lo) + d(a_lo, b_hi))


def _norm_proj_kernel(x_ref, g_ref, w_ref, q_ref, k_ref, v_ref, p_ref):
    h = _rms(x_ref[...], g_ref[...])
    proj = jnp.dot(h.astype(BF16), w_ref[...], preferred_element_type=F32)
    w = q_ref.shape[1]
    q_ref[...] = proj[:, 0 * w:1 * w]
    k_ref[...] = proj[:, 1 * w:2 * w]
    v_ref[...] = proj[:, 2 * w:3 * w]
    p_ref[...] = proj[:, 3 * w:4 * w]


def _norm_proj(x, g, w16, tm):
    t, d = x.shape
    w = w16.shape[1] // 4
    row = pl.BlockSpec((tm, d), lambda i: (i, 0))
    out = pl.BlockSpec((tm, w), lambda i: (i, 0))
    return pl.pallas_call(
        _norm_proj_kernel,
        grid=(t // tm,),
        in_specs=[row, pl.BlockSpec((1, d), lambda i: (0, 0)), pl.BlockSpec(w16.shape, lambda i: (0, 0))],
        out_specs=[out] * 4,
        out_shape=[jax.ShapeDtypeStruct((t, w), F32)] * 4,
        compiler_params=_params(("parallel",)),
        name="norm_proj",
    )(x, g.reshape(1, d), w16)


def _block_rank_select(gate, n_cand):
    lane = lax.broadcasted_iota(jnp.int32, gate.shape, 1)
    rank = jnp.zeros(gate.shape, F32)
    for m in range(n_cand):
        gm = gate[:, m:m + 1]
        beats = jnp.where(gm > gate, 1.0, jnp.where(gm == gate, jnp.where(lane > m, 1.0, 0.0), 0.0))
        rank = rank + beats
    return jnp.where(lane < n_cand, rank, float(MOBA_TOPK)) < float(MOBA_TOPK)


def _moba_prompt_kernel(q_ref, k_ref, v_ref, o_ref):
    s_len = q_ref.shape[1]
    nb = s_len // MOBA_BLOCK
    k = k_ref[0]
    k16 = k.astype(BF16)
    v16 = v_ref[0].astype(BF16)
    lane = lax.broadcasted_iota(jnp.int32, (1, LANES), 1)
    head_lanes = [lane < HEAD_DIM, lane >= HEAD_DIM]
    means = [jnp.mean(k[n * MOBA_BLOCK:(n + 1) * MOBA_BLOCK], axis=0, keepdims=True) for n in range(nb)]
    kbar = jnp.concatenate(means + [jnp.zeros((LANES - nb, LANES), F32)], axis=0)
    row = lax.broadcasted_iota(jnp.int32, (MOBA_BLOCK, MOBA_BLOCK), 0)
    col = lax.broadcasted_iota(jnp.int32, (MOBA_BLOCK, MOBA_BLOCK), 1)
    causal = row >= col
    scale = HEAD_DIM ** -0.5
    for i in range(nb):
        q = q_ref[0, i * MOBA_BLOCK:(i + 1) * MOBA_BLOCK, :]
        n_keys = (i + 1) * MOBA_BLOCK
        outs = []
        for hh in range(2):
            qh = jnp.where(head_lanes[hh], q, 0.0)
            s = lax.dot_general((qh * scale).astype(BF16), k16[:n_keys], _NT, preferred_element_type=F32)
            pieces = []
            if i > 0:
                gate = _dot_nt_3pass(qh, kbar)
                sel = jnp.where(_block_rank_select(gate, i), 1.0, 0.0)
                pieces = [jnp.where(sel[:, n:n + 1] > 0.5, s[:, n * MOBA_BLOCK:(n + 1) * MOBA_BLOCK], NEG)
                          for n in range(i)]
            pieces.append(jnp.where(causal, s[:, i * MOBA_BLOCK:], NEG))
            s = jnp.concatenate(pieces, axis=1)
            m = jnp.max(s, axis=-1, keepdims=True)
            p = jnp.exp(s - m)
            l = jnp.sum(p, axis=-1, keepdims=True)
            o = jnp.dot(p.astype(BF16), v16[:n_keys], preferred_element_type=F32)
            outs.append(o / l)
        o_ref[0, i * MOBA_BLOCK:(i + 1) * MOBA_BLOCK, :] = jnp.where(head_lanes[0], outs[0], outs[1])


def _moba_prompt(q, k, v):
    b, s, w = q.shape
    spec = pl.BlockSpec((1, s, LANES), lambda bi, pi: (bi, 0, pi))
    return pl.pallas_call(
        _moba_prompt_kernel,
        grid=(b, w // LANES),
        in_specs=[spec] * 3,
        out_specs=spec,
        out_shape=jax.ShapeDtypeStruct((b, s, w), F32),
        compiler_params=_params(("parallel", "parallel")),
        name="moba_prompt",
    )(q, k, v)


def _moba_sample_kernel(pt_ref, q_ref, kn_ref, vn_ref, *refs, blocks_per_step, pages_per_block):
    del pt_ref
    n_pg = blocks_per_step * pages_per_block
    kt_refs, vt_refs = refs[:n_pg], refs[n_pg:2 * n_pg]
    o_ref, m_scr, l_scr, g_scr, o_scr = refs[2 * n_pg:]
    step = pl.program_id(1)
    tq, w = q_ref.shape[1], q_ref.shape[2]
    n_heads = w // HEAD_DIM
    rows = tq * n_heads
    scale = HEAD_DIM ** -0.5

    head_of_lane = lax.broadcasted_iota(jnp.int32, (n_heads, w), 1) // HEAD_DIM
    head_mask = jnp.where(head_of_lane == lax.broadcasted_iota(jnp.int32, (n_heads, w), 0), 1.0, 0.0)
    q = q_ref[0]
    qbd = jnp.concatenate([jnp.broadcast_to(q[j:j + 1], (n_heads, w)) * head_mask for j in range(tq)], axis=0)

    @pl.when(step == 0)
    def _():
        m_scr[...] = jnp.zeros_like(m_scr)
        l_scr[...] = jnp.zeros_like(l_scr)
        g_scr[...] = jnp.zeros_like(g_scr)

    qs16 = (qbd * scale).astype(BF16)
    blk_lane = lax.broadcasted_iota(jnp.int32, (rows, LANES), 1)
    m_all, l_all, g_all = m_scr[...], l_scr[...], g_scr[...]
    for jj in range(blocks_per_step):
        n = step * blocks_per_step + jj
        pgs = range(jj * pages_per_block, (jj + 1) * pages_per_block)
        s = jnp.concatenate([jnp.dot(qs16, kt_refs[pg][0, 0].astype(BF16), preferred_element_type=F32)
                             for pg in pgs], axis=1)
        m_n = jnp.max(s, axis=-1, keepdims=True)
        p = jnp.exp(s - m_n)
        l_n = jnp.sum(p, axis=-1, keepdims=True)
        g_n = jnp.sum(s, axis=-1, keepdims=True)
        p16 = p.astype(BF16)
        page = kt_refs[0].shape[3]
        o_n = None
        for i, pg in enumerate(pgs):
            part = lax.dot_general(p16[:, i * page:(i + 1) * page], vt_refs[pg][0, 0].astype(BF16), _NT,
                                   preferred_element_type=F32)
            o_n = part if o_n is None else o_n + part
        o_scr[n] = o_n
        here = blk_lane == n
        m_all = jnp.where(here, m_n, m_all)
        l_all = jnp.where(here, l_n, l_all)
        g_all = jnp.where(here, g_n, g_all)
    m_scr[...] = m_all
    l_scr[...] = l_all
    g_scr[...] = g_all

    @pl.when(step == pl.num_programs(1) - 1)
    def _():
        n_past = o_scr.shape[0]
        sel = _block_rank_select(g_scr[...], n_past)
        m_all = m_scr[...]
        m_run = jnp.max(jnp.where(sel, m_all, NEG), axis=-1, keepdims=True)
        q_idx = lax.broadcasted_iota(jnp.int32, (rows, 1), 0) // n_heads
        kn = kn_ref[0]
        vn = vn_ref[0]
        s_own = []
        for j in range(tq):
            sj = jnp.sum(qbd * scale * kn[j:j + 1], axis=-1, keepdims=True)
            s_own.append(jnp.where(q_idx >= j, sj, NEG))
            m_run = jnp.maximum(m_run, s_own[j])
        wgt = jnp.where(sel, jnp.exp(m_all - m_run), 0.0)
        den = jnp.sum(wgt * l_scr[...], axis=-1, keepdims=True)
        num = jnp.zeros((rows, w), F32)
        for nn in range(n_past):
            num = num + wgt[:, nn:nn + 1] * o_scr[nn]
        for j in range(tq):
            e = jnp.where(q_idx >= j, jnp.exp(s_own[j] - m_run), 0.0)
            den = den + e
            num = num + e * vn[j:j + 1]
        acc = num / den
        out_rows = [jnp.sum(acc[j * n_heads:(j + 1) * n_heads] * head_mask, axis=0, keepdims=True)
                    for j in range(tq)]
        o_ref[0] = jnp.concatenate(out_rows, axis=0)


SAMPLE_BLOCKS_PER_STEP = 4


def _moba_sample(q, k_new, v_new, cache_kt, cache_vt, page_table, layer):
    db, tq, w = q.shape
    page = cache_kt.shape[3]
    n_pages = page_table.shape[1]
    ppb = MOBA_BLOCK // page
    assert MOBA_BLOCK == ppb * page and n_pages % (ppb * SAMPLE_BLOCKS_PER_STEP) == 0
    n_past = n_pages // ppb
    assert n_past <= LANES and tq <= MOBA_BLOCK
    rows = tq * (w // HEAD_DIM)
    n_pg = SAMPLE_BLOCKS_PER_STEP * ppb
    tok = pl.BlockSpec((1, tq, w), lambda b, s, pt: (b, 0, 0))

    def page_spec(which):
        return pl.BlockSpec((1, 1, w, page), lambda b, s, pt: (pt[b, s * n_pg + which], layer, 0, 0))

    pages = [page_spec(i) for i in range(n_pg)]
    grid_spec = pltpu.PrefetchScalarGridSpec(
        num_scalar_prefetch=1,
        grid=(db, n_past // SAMPLE_BLOCKS_PER_STEP),
        in_specs=[tok, tok, tok] + pages + pages,
        out_specs=tok,
        scratch_shapes=[pltpu.VMEM((rows, LANES), F32)] * 3 + [pltpu.VMEM((n_past, rows, w), F32)],
    )
    return pl.pallas_call(
        functools.partial(_moba_sample_kernel, blocks_per_step=SAMPLE_BLOCKS_PER_STEP, pages_per_block=ppb),
        grid_spec=grid_spec,
        out_shape=jax.ShapeDtypeStruct((db, tq, w), F32),
        compiler_params=_params(("parallel", "arbitrary")),
        name="moba_sample",
    )(page_table, q, k_new, v_new, *([cache_kt] * n_pg), *([cache_vt] * n_pg))


def _pool_mix_kernel(x_ref, pre_ref, w_ref, sc_ref, o_ref, xp_scr, *, pos0):
    t = x_ref.shape[1]
    gw = w_ref.shape[1]
    xp_scr[0:POOL_PREFIX, :] = pre_ref[0]
    xp_scr[POOL_PREFIX:POOL_PREFIX + t, :] = x_ref[0]
    pos = (lax.broadcasted_iota(jnp.int32, (t, 1), 0) + (pos0 + 1)).astype(F32)
    outs = []
    for g, win in enumerate(POOL_WINDOWS):
        sl = slice(g * gw, (g + 1) * gw)
        cur = xp_scr[POOL_PREFIX:POOL_PREFIX + t, sl]
        acc = cur
        for j in range(1, win):
            acc = acc + xp_scr[POOL_PREFIX - j:POOL_PREFIX - j + t, sl]
        r = acc / jnp.minimum(pos, float(win)) - cur
        outs.append(jnp.dot(r.astype(BF16), w_ref[g], preferred_element_type=F32))
    o_ref[0] = jnp.concatenate(outs, axis=1) * sc_ref[...]


def _pool_mix(pin, prefix, w_pool16, scale, pos0):
    b, t, w = pin.shape
    seq = pl.BlockSpec((1, t, w), lambda i: (i, 0, 0))
    return pl.pallas_call(
        functools.partial(_pool_mix_kernel, pos0=pos0),
        grid=(b,),
        in_specs=[seq, pl.BlockSpec((1, POOL_PREFIX, w), lambda i: (i, 0, 0)),
                  pl.BlockSpec(w_pool16.shape, lambda i: (0, 0, 0)), pl.BlockSpec((1, w), lambda i: (0, 0))],
        out_specs=seq,
        out_shape=jax.ShapeDtypeStruct((b, t, w), F32),
        scratch_shapes=[pltpu.VMEM((POOL_PREFIX + t, w), F32)],
        compiler_params=_params(("parallel",)),
        name="pool_mix",
    )(pin, prefix, w_pool16, scale.reshape(1, w))


def _mix_out_kernel(x_ref, a_ref, p_ref, wo_ref, g_ref, wq_ref, keys_ref, x1_ref, h_ref, s1_ref, s2_ref):
    aw = a_ref.shape[1]
    mix = (jnp.dot(a_ref[...].astype(BF16), wo_ref[0:aw, :], preferred_element_type=F32)
           + jnp.dot(p_ref[...].astype(BF16), wo_ref[aw:, :], preferred_element_type=F32))
    x1 = x_ref[...] + mix
    x1_ref[...] = x1
    h16 = _rms(x1, g_ref[...]).astype(BF16)
    h_ref[...] = h16
    qq = jnp.dot(h16, wq_ref[...], preferred_element_type=F32)
    half = keys_ref.shape[2]
    for hd in range(PEER_HEADS):
        q1 = qq[:, (2 * hd) * half:(2 * hd + 1) * half]
        q2 = qq[:, (2 * hd + 1) * half:(2 * hd + 2) * half]
        s1_ref[hd] = _dot_nt_3pass(keys_ref[0], q1)
        s2_ref[hd] = _dot_nt_3pass(keys_ref[1], q2)


def _mix_out(x, attn, pool, wo16, g, wq16, keys, tm):
    t, d = x.shape
    aw = attn.shape[1]
    nk = keys.shape[1]
    row = lambda wdt: pl.BlockSpec((tm, wdt), lambda i: (i, 0))
    full = lambda a: pl.BlockSpec(a.shape, lambda i: (0,) * a.ndim)
    sc = pl.BlockSpec((PEER_HEADS, nk, tm), lambda i: (0, 0, i))
    return pl.pallas_call(
        _mix_out_kernel,
        grid=(t // tm,),
        in_specs=[row(d), row(aw), row(aw), full(wo16), pl.BlockSpec((1, d), lambda i: (0, 0)), full(wq16), full(keys)],
        out_specs=[row(d), row(d), sc, sc],
        out_shape=[jax.ShapeDtypeStruct((t, d), F32), jax.ShapeDtypeStruct((t, d), BF16),
                   jax.ShapeDtypeStruct((PEER_HEADS, nk, t), F32), jax.ShapeDtypeStruct((PEER_HEADS, nk, t), F32)],
        compiler_params=_params(("parallel",)),
        name="mix_out",
    )(x, attn, pool, wo16, g.reshape(1, d), wq16, keys)


def _top_values(s, count):
    vals = []
    cur = s
    for _ in range(count):
        mx = jnp.max(cur, axis=0, keepdims=True)
        vals.append(mx)
        cur = jnp.where(cur == mx, NEG, cur)
    return vals


def _peer_gate_kernel(s1_ref, s2_ref, e1_ref, e2_ref, thr_ref):
    s1 = s1_ref[0]
    s2 = s2_ref[0]
    v1 = _top_values(s1, PEER_TOPK)
    v2 = _top_values(s2, PEER_TOPK)
    pairs = [(a, b) for a in range(PEER_TOPK) for b in range(PEER_TOPK // (a + 1))]
    pad = [jnp.full_like(v1[0], NEG)] * (-len(pairs) % 8)
    cand = jnp.concatenate([v1[a] + v2[b] for a, b in pairs] + pad, axis=0)
    tau = _top_values(cand, PEER_TOPK)[-1]
    cmax = v1[0] + v2[0]
    chosen = cand >= tau
    z = jnp.sum(jnp.where(chosen, jnp.exp(cand - cmax), 0.0), axis=0, keepdims=True)
    e1 = lambda s: (jnp.exp(s - v1[0]) / z).astype(BF16)
    e2 = lambda s: jnp.exp(s - v2[0]).astype(BF16)
    e1_ref[0] = jnp.where(s1 >= v1[-1], e1(s1).astype(F32), 0.0)
    e2_ref[0] = jnp.where(s2 >= v2[-1], e2(s2), jnp.zeros_like(s2, BF16))
    v2m = jnp.concatenate(v2, axis=0)
    e2m = e2(v2m)
    thr = None
    for a in range(PEER_TOPK):
        prod = (e1(jnp.broadcast_to(v1[a], v2m.shape)) * e2m).astype(F32)
        low = jnp.min(jnp.where(v1[a] + v2m >= tau, prod, 1.0), axis=0, keepdims=True)
        thr = low if thr is None else jnp.minimum(thr, low)
    thr_ref[0] = thr


def _peer_gate(s1t, s2t, tl):
    nh, nk, t = s1t.shape
    spec = pl.BlockSpec((1, nk, tl), lambda h, i: (h, 0, i))
    tspec = pl.BlockSpec((1, 1, tl), lambda h, i: (h, 0, i))
    return pl.pallas_call(
        _peer_gate_kernel,
        grid=(nh, t // tl),
        in_specs=[spec, spec],
        out_specs=[spec, spec, tspec],
        out_shape=[jax.ShapeDtypeStruct((nh, nk, t), F32), jax.ShapeDtypeStruct((nh, nk, t), BF16),
                   jax.ShapeDtypeStruct((nh, 1, t), F32)],
        compiler_params=_params(("parallel", "parallel")),
        name="peer_gate",
    )(s1t, s2t)


PEER_SUB_BLOCK = 256


def _peer_dense_kernel(h_ref, x_ref, u_ref, vt_ref, e1_ref, e2_ref, thr_ref, y_ref, a_scr, w_scr, acc_scr):
    j = pl.program_id(1)
    tt = h_ref.shape[0]
    nk = e2_ref.shape[1]
    eb = u_ref.shape[0]

    @pl.when(j == 0)
    def _():
        acc_scr[...] = jnp.zeros_like(acc_scr)

    h = h_ref[...]
    zero = jnp.zeros((nk, LANES), BF16)
    n_sub = eb // PEER_SUB_BLOCK

    def pre_activations(sb):
        sub = slice(sb * PEER_SUB_BLOCK, (sb + 1) * PEER_SUB_BLOCK)
        a_scr[sub, :] = lax.dot_general(u_ref[sub, :], h, _NT, preferred_element_type=F32)

    def gated_weights(sb):
        for c in range(sb * PEER_SUB_BLOCK // nk, (sb + 1) * PEER_SUB_BLOCK // nk):
            rows = slice(c * nk, (c + 1) * nk)
            for lt in range(tt // LANES):
                ls = slice(lt * LANES, (lt + 1) * LANES)
                gate = None
                for hd in range(PEER_HEADS):
                    e1 = jnp.broadcast_to(e1_ref[hd, c:c + 1, ls], (nk, LANES)).astype(BF16)
                    thr = jnp.broadcast_to(thr_ref[hd, :, ls], (nk, LANES)).astype(BF16)
                    p = e1 * e2_ref[hd, :, ls]
                    g = jnp.where(p >= thr, p, zero)
                    gate = g if gate is None else gate + g
                a = a_scr[rows, ls]
                gelu = 0.5 * a * (1.0 + lax.erf(a * (2.0 ** -0.5)))
                w_scr[rows, ls] = gate * gelu.astype(BF16)

    pre_activations(0)
    for sb in range(n_sub):
        if sb + 1 < n_sub:
            pre_activations(sb + 1)
        gated_weights(sb)
        sub = slice(sb * PEER_SUB_BLOCK, (sb + 1) * PEER_SUB_BLOCK)
        acc_scr[...] += jnp.dot(vt_ref[:, sub], w_scr[sub, :], preferred_element_type=F32)

    @pl.when(j == pl.num_programs(1) - 1)
    def _():
        y_ref[...] = x_ref[...] + acc_scr[...].T


def _peer_dense(h16, x, u16, vt16, x1t, m2t, tau, tt, eb):
    t, d = x.shape
    e = u16.shape[0]
    nh, nk, _ = x1t.shape
    assert eb % PEER_SUB_BLOCK == 0 and PEER_SUB_BLOCK % nk == 0
    tok = lambda: pl.BlockSpec((tt, d), lambda i, j: (i, 0))
    return pl.pallas_call(
        _peer_dense_kernel,
        grid=(t // tt, e // eb),
        in_specs=[tok(), tok(),
                  pl.BlockSpec((eb, d), lambda i, j: (j, 0)),
                  pl.BlockSpec((d, eb), lambda i, j: (0, j)),
                  pl.BlockSpec((nh, eb // nk, tt), lambda i, j: (0, j, i)),
                  pl.BlockSpec((nh, nk, tt), lambda i, j: (0, 0, i)),
                  pl.BlockSpec((nh, 1, tt), lambda i, j: (0, 0, i))],
        out_specs=tok(),
        out_shape=jax.ShapeDtypeStruct((t, d), F32),
        scratch_shapes=[pltpu.VMEM((eb, tt), F32), pltpu.VMEM((eb, tt), BF16), pltpu.VMEM((d, tt), F32)],
        compiler_params=_params(("parallel", "arbitrary")),
        name="peer_dense",
    )(h16, x, u16, vt16, x1t, m2t, tau)


def _final_norm_kernel(x_ref, g_ref, o_ref):
    o_ref[...] = _rms(x_ref[...], g_ref[...])


def _final_norm(x, g, tm):
    t, d = x.shape
    row = pl.BlockSpec((tm, d), lambda i: (i, 0))
    return pl.pallas_call(
        _final_norm_kernel,
        grid=(t // tm,),
        in_specs=[row, pl.BlockSpec((1, d), lambda i: (0, 0))],
        out_specs=row,
        out_shape=jax.ShapeDtypeStruct((t, d), F32),
        compiler_params=_params(("parallel",)),
        name="final_norm",
    )(x, g.reshape(1, d))


def _token_tile(t, want):
    return want if t % want == 0 else t


def _ffn(x, attn, pool, wo16, g_ffn, wq16, keys, u16, vt16):
    t = x.shape[0]
    x1, h16, s1t, s2t = _mix_out(x, attn, pool, wo16, g_ffn, wq16, keys, _token_tile(t, 256))
    x1t, m2t, tau = _peer_gate(s1t, s2t, _token_tile(t, 512))
    return _peer_dense(h16, x1, u16, vt16, x1t, m2t, tau, _token_tile(t, 512), 1024)


def kernel(x_prompt, x_sample, cache_k, cache_v, state_pool, page_table, norm_mix, w_in, w_pool, pool_scale,
           w_out, norm_ffn, peer_wq, peer_keys, peer_u, peer_v, norm_final):
    bp, sp, d = x_prompt.shape
    db, ts, _ = x_sample.shape
    depth = w_in.shape[0]
    aw = w_in.shape[2] // 4
    n_heads = aw // HEAD_DIM
    pool_buf = state_pool.shape[2]
    assert pool_buf == POOL_PREFIX - 1 and sp % MOBA_BLOCK == 0 and aw % LANES == 0
    n_pool, _, page, _, _ = cache_k.shape
    ck = cache_k.transpose(0, 1, 3, 4, 2).reshape(n_pool, depth, aw, page)
    cv = cache_v.transpose(0, 1, 3, 4, 2).reshape(n_pool, depth, aw, page)
    past_len = page_table.shape[1] * page
    ts_pad = -(-ts // 8) * 8

    y_p = x_prompt.reshape(bp * sp, d)
    y_s = x_sample.reshape(db * ts, d)
    kp, vp, pp, kss, vss, pss = [], [], [], [], [], []
    for l in range(depth):
        w_in16 = w_in[l].astype(BF16)
        w_pool16 = w_pool[l].astype(BF16)
        wo16 = w_out[l].astype(BF16)
        wq16 = peer_wq[l].astype(BF16)
        u16 = peer_u[l].astype(BF16)
        vt16 = peer_v[l].T.astype(BF16)
        q, k, v, pin = _norm_proj(y_p, norm_mix[l], w_in16, 512)
        q3, k3, v3, pin3 = (a.reshape(bp, sp, aw) for a in (q, k, v, pin))
        attn = _moba_prompt(q3, k3, v3)
        pool = _pool_mix(pin3, jnp.zeros((bp, POOL_PREFIX, aw), F32), w_pool16, pool_scale[l], 0)
        y_p = _ffn(y_p, attn.reshape(bp * sp, aw), pool.reshape(bp * sp, aw), wo16, norm_ffn[l], wq16,
                   peer_keys[l], u16, vt16)
        kp.append(k3.reshape(bp, sp, n_heads, HEAD_DIM))
        vp.append(v3.reshape(bp, sp, n_heads, HEAD_DIM))
        pp.append(pin3[:, sp - pool_buf:])
        q, k, v, pin = _norm_proj(y_s, norm_mix[l], w_in16, db * ts)
        q3, k3, v3, pin3 = (a.reshape(db, ts, aw) for a in (q, k, v, pin))
        attn = _moba_sample(q3, k3, v3, ck, cv, page_table, l)
        prefix = jnp.concatenate([jnp.zeros((db, 1, aw), F32), state_pool[l]], axis=1)
        pin_pad = jnp.pad(pin3, ((0, 0), (0, ts_pad - ts), (0, 0)))
        pool = _pool_mix(pin_pad, prefix, w_pool16, pool_scale[l], past_len)[:, :ts]
        y_s = _ffn(y_s, attn.reshape(db * ts, aw), pool.reshape(db * ts, aw), wo16, norm_ffn[l], wq16,
                   peer_keys[l], u16, vt16)
        kss.append(k3.reshape(db, ts, n_heads, HEAD_DIM))
        vss.append(v3.reshape(db, ts, n_heads, HEAD_DIM))
        pss.append(jnp.concatenate([state_pool[l], pin3], axis=1)[:, ts:])
    y_prompt = _final_norm(y_p, norm_final, 512).reshape(bp, sp, d)
    y_sample = _final_norm(y_s, norm_final, db * ts).reshape(db, ts, d)
    return (y_prompt, y_sample, jnp.stack(kp, axis=1), jnp.stack(vp, axis=1), jnp.stack(pp, axis=0),
            jnp.stack(kss, axis=1), jnp.stack(vss, axis=1), jnp.stack(pss, axis=0))
```

```python
import functools
import math

import jax
import jax.numpy as jnp
from jax import lax
from jax.experimental import pallas as pl
from jax.experimental.pallas import tpu as pltpu

F32 = jnp.float32
BF16 = jnp.bfloat16

HEAD_DIM = 64
MOBA_BLOCK = 256
MOBA_TOPK = 3
POOL_WINDOWS = (2, 4, 8, 16)
POOL_PREFIX = 16
PEER_HEADS = 8
PEER_NKEYS = 128
PEER_TOPK = 16
RMS_EPS = 1e-6
NEG = -1e30
LOG2E = 1.4426950408889634

LANES = 128
VMEM_LIMIT = 56 * 1024 * 1024

_NT = (((1,), (1,)), ((), ()))


def _params(semantics):
    return pltpu.CompilerParams(dimension_semantics=semantics, vmem_limit_bytes=VMEM_LIMIT)


def _rms(x, g):
    return x * lax.rsqrt(jnp.mean(x * x, axis=-1, keepdims=True) + RMS_EPS) * g


def _split_bf16(x):
    hi = x.astype(BF16)
    lo = (x - hi.astype(F32)).astype(BF16)
    return hi, lo


def _dot_nt_3pass(a, b):
    a_hi, a_lo = _split_bf16(a)
    b_hi, b_lo = _split_bf16(b)
    d = functools.partial(lax.dot_general, dimension_numbers=_NT, preferred_element_type=F32)
    return d(a_hi, b_hi) + (d(a_hi, b_lo) + d(a_lo, b_hi))


def _norm_proj_kernel(x_ref, g_ref, w_ref, q_ref, k_ref, v_ref, p_ref):
    h = _rms(x_ref[...], g_ref[...])
    proj = jnp.dot(h.astype(BF16), w_ref[...], preferred_element_type=F32)
    w = q_ref.shape[1]
    q_ref[...] = proj[:, 0 * w:1 * w]
    k_ref[...] = proj[:, 1 * w:2 * w]
    v_ref[...] = proj[:, 2 * w:3 * w]
    p_ref[...] = proj[:, 3 * w:4 * w]


def _norm_proj(x, g, w16, tm):
    t, d = x.shape
    w = w16.shape[1] // 4
    row = pl.BlockSpec((tm, d), lambda i: (i, 0))
    out = pl.BlockSpec((tm, w), lambda i: (i, 0))
    return pl.pallas_call(
        _norm_proj_kernel,
        grid=(t // tm,),
        in_specs=[row, pl.BlockSpec((1, d), lambda i: (0, 0)), pl.BlockSpec(w16.shape, lambda i: (0, 0))],
        out_specs=[out] * 4,
        out_shape=[jax.ShapeDtypeStruct((t, w), F32)] * 4,
        compiler_params=_params(("parallel",)),
        name="norm_proj",
    )(x, g.reshape(1, d), w16)


def _block_rank_select(gate, n_cand):
    lane = lax.broadcasted_iota(jnp.int32, gate.shape, 1)
    rank = jnp.zeros(gate.shape, F32)
    for m in range(n_cand):
        gm = gate[:, m:m + 1]
        beats = jnp.where(gm > gate, 1.0, jnp.where(gm == gate, jnp.where(lane > m, 1.0, 0.0), 0.0))
        rank = rank + beats
    return jnp.where(lane < n_cand, rank, float(MOBA_TOPK)) < float(MOBA_TOPK)


def _moba_prompt_kernel(q_ref, k_ref, v_ref, o_ref):
    s_len = q_ref.shape[1]
    nb = s_len // MOBA_BLOCK
    k = k_ref[0]
    k16 = k.astype(BF16)
    v16 = v_ref[0].astype(BF16)
    lane = lax.broadcasted_iota(jnp.int32, (1, LANES), 1)
    head_lanes = [lane < HEAD_DIM, lane >= HEAD_DIM]
    means = [jnp.mean(k[n * MOBA_BLOCK:(n + 1) * MOBA_BLOCK], axis=0, keepdims=True) for n in range(nb)]
    kbar = jnp.concatenate(means + [jnp.zeros((LANES - nb, LANES), F32)], axis=0)
    row = lax.broadcasted_iota(jnp.int32, (MOBA_BLOCK, MOBA_BLOCK), 0)
    col = lax.broadcasted_iota(jnp.int32, (MOBA_BLOCK, MOBA_BLOCK), 1)
    causal = row >= col
    scale = HEAD_DIM ** -0.5
    for i in range(nb):
        q = q_ref[0, i * MOBA_BLOCK:(i + 1) * MOBA_BLOCK, :]
        n_keys = (i + 1) * MOBA_BLOCK
        outs = []
        for hh in range(2):
            qh = jnp.where(head_lanes[hh], q, 0.0)
            s = lax.dot_general((qh * scale).astype(BF16), k16[:n_keys], _NT, preferred_element_type=F32)
            pieces = []
            if i > 0:
                gate = _dot_nt_3pass(qh, kbar)
                sel = jnp.where(_block_rank_select(gate, i), 1.0, 0.0)
                pieces = [jnp.where(sel[:, n:n + 1] > 0.5, s[:, n * MOBA_BLOCK:(n + 1) * MOBA_BLOCK], NEG)
                          for n in range(i)]
            pieces.append(jnp.where(causal, s[:, i * MOBA_BLOCK:], NEG))
            s = jnp.concatenate(pieces, axis=1)
            m = jnp.max(s, axis=-1, keepdims=True)
            p = jnp.exp(s - m)
            l = jnp.sum(p, axis=-1, keepdims=True)
            o = jnp.dot(p.astype(BF16), v16[:n_keys], preferred_element_type=F32)
            outs.append(o / l)
        o_ref[0, i * MOBA_BLOCK:(i + 1) * MOBA_BLOCK, :] = jnp.where(head_lanes[0], outs[0], outs[1])


def _moba_prompt(q, k, v):
    b, s, w = q.shape
    spec = pl.BlockSpec((1, s, LANES), lambda bi, pi: (bi, 0, pi))
    return pl.pallas_call(
        _moba_prompt_kernel,
        grid=(b, w // LANES),
        in_specs=[spec] * 3,
        out_specs=spec,
        out_shape=jax.ShapeDtypeStruct((b, s, w), F32),
        compiler_params=_params(("parallel", "parallel")),
        name="moba_prompt",
    )(q, k, v)


def _moba_sample_kernel(pt_ref, q_ref, kn_ref, vn_ref, *refs, blocks_per_step, pages_per_block):
    del pt_ref
    n_pg = blocks_per_step * pages_per_block
    kt_refs, vt_refs = refs[:n_pg], refs[n_pg:2 * n_pg]
    o_ref, m_scr, l_scr, g_scr, o_scr = refs[2 * n_pg:]
    step = pl.program_id(1)
    tq, w = q_ref.shape[1], q_ref.shape[2]
    n_heads = w // HEAD_DIM
    rows = tq * n_heads
    scale = HEAD_DIM ** -0.5

    head_of_lane = lax.broadcasted_iota(jnp.int32, (n_heads, w), 1) // HEAD_DIM
    head_mask = jnp.where(head_of_lane == lax.broadcasted_iota(jnp.int32, (n_heads, w), 0), 1.0, 0.0)
    q = q_ref[0]
    qbd = jnp.concatenate([jnp.broadcast_to(q[j:j + 1], (n_heads, w)) * head_mask for j in range(tq)], axis=0)

    @pl.when(step == 0)
    def _():
        m_scr[...] = jnp.zeros_like(m_scr)
        l_scr[...] = jnp.zeros_like(l_scr)
        g_scr[...] = jnp.zeros_like(g_scr)

    qs16 = (qbd * scale).astype(BF16)
    blk_lane = lax.broadcasted_iota(jnp.int32, (rows, LANES), 1)
    m_all, l_all, g_all = m_scr[...], l_scr[...], g_scr[...]
    for jj in range(blocks_per_step):
        n = step * blocks_per_step + jj
        pgs = range(jj * pages_per_block, (jj + 1) * pages_per_block)
        s = jnp.concatenate([jnp.dot(qs16, kt_refs[pg][0, 0].astype(BF16), preferred_element_type=F32)
                             for pg in pgs], axis=1)
        m_n = jnp.max(s, axis=-1, keepdims=True)
        p = jnp.exp(s - m_n)
        l_n = jnp.sum(p, axis=-1, keepdims=True)
        g_n = jnp.sum(s, axis=-1, keepdims=True)
        p16 = p.astype(BF16)
        page = kt_refs[0].shape[3]
        o_n = None
        for i, pg in enumerate(pgs):
            part = lax.dot_general(p16[:, i * page:(i + 1) * page], vt_refs[pg][0, 0].astype(BF16), _NT,
                                   preferred_element_type=F32)
            o_n = part if o_n is None else o_n + part
        o_scr[n] = o_n
        here = blk_lane == n
        m_all = jnp.where(here, m_n, m_all)
        l_all = jnp.where(here, l_n, l_all)
        g_all = jnp.where(here, g_n, g_all)
    m_scr[...] = m_all
    l_scr[...] = l_all
    g_scr[...] = g_all

    @pl.when(step == pl.num_programs(1) - 1)
    def _():
        n_past = o_scr.shape[0]
        sel = _block_rank_select(g_scr[...], n_past)
        m_all = m_scr[...]
        m_run = jnp.max(jnp.where(sel, m_all, NEG), axis=-1, keepdims=True)
        q_idx = lax.broadcasted_iota(jnp.int32, (rows, 1), 0) // n_heads
        kn = kn_ref[0]
        vn = vn_ref[0]
        s_own = []
        for j in range(tq):
            sj = jnp.sum(qbd * scale * kn[j:j + 1], axis=-1, keepdims=True)
            s_own.append(jnp.where(q_idx >= j, sj, NEG))
            m_run = jnp.maximum(m_run, s_own[j])
        wgt = jnp.where(sel, jnp.exp(m_all - m_run), 0.0)
        den = jnp.sum(wgt * l_scr[...], axis=-1, keepdims=True)
        num = jnp.zeros((rows, w), F32)
        for nn in range(n_past):
            num = num + wgt[:, nn:nn + 1] * o_scr[nn]
        for j in range(tq):
            e = jnp.where(q_idx >= j, jnp.exp(s_own[j] - m_run), 0.0)
            den = den + e
            num = num + e * vn[j:j + 1]
        acc = num / den
        out_rows = [jnp.sum(acc[j * n_heads:(j + 1) * n_heads] * head_mask, axis=0, keepdims=True)
                    for j in range(tq)]
        o_ref[0] = jnp.concatenate(out_rows, axis=0)


SAMPLE_BLOCKS_PER_STEP = 4


def _moba_sample(q, k_new, v_new, cache_kt, cache_vt, page_table, layer):
    db, tq, w = q.shape
    page = cache_kt.shape[3]
    n_pages = page_table.shape[1]
    ppb = MOBA_BLOCK // page
    assert MOBA_BLOCK == ppb * page and n_pages % (ppb * SAMPLE_BLOCKS_PER_STEP) == 0
    n_past = n_pages // ppb
    assert n_past <= LANES and tq <= MOBA_BLOCK
    rows = tq * (w // HEAD_DIM)
    n_pg = SAMPLE_BLOCKS_PER_STEP * ppb
    tok = pl.BlockSpec((1, tq, w), lambda b, s, pt: (b, 0, 0))

    def page_spec(which):
        return pl.BlockSpec((1, 1, w, page), lambda b, s, pt: (pt[b, s * n_pg + which], layer, 0, 0))

    pages = [page_spec(i) for i in range(n_pg)]
    grid_spec = pltpu.PrefetchScalarGridSpec(
        num_scalar_prefetch=1,
        grid=(db, n_past // SAMPLE_BLOCKS_PER_STEP),
        in_specs=[tok, tok, tok] + pages + pages,
        out_specs=tok,
        scratch_shapes=[pltpu.VMEM((rows, LANES), F32)] * 3 + [pltpu.VMEM((n_past, rows, w), F32)],
    )
    return pl.pallas_call(
        functools.partial(_moba_sample_kernel, blocks_per_step=SAMPLE_BLOCKS_PER_STEP, pages_per_block=ppb),
        grid_spec=grid_spec,
        out_shape=jax.ShapeDtypeStruct((db, tq, w), F32),
        compiler_params=_params(("parallel", "arbitrary")),
        name="moba_sample",
    )(page_table, q, k_new, v_new, *([cache_kt] * n_pg), *([cache_vt] * n_pg))


def _pool_mix_kernel(x_ref, pre_ref, w_ref, sc_ref, o_ref, xp_scr, *, pos0):
    t = x_ref.shape[1]
    gw = w_ref.shape[1]
    xp_scr[0:POOL_PREFIX, :] = pre_ref[0]
    xp_scr[POOL_PREFIX:POOL_PREFIX + t, :] = x_ref[0]
    pos = (lax.broadcasted_iota(jnp.int32, (t, 1), 0) + (pos0 + 1)).astype(F32)
    outs = []
    for g, win in enumerate(POOL_WINDOWS):
        sl = slice(g * gw, (g + 1) * gw)
        cur = xp_scr[POOL_PREFIX:POOL_PREFIX + t, sl]
        acc = cur
        for j in range(1, win):
            acc = acc + xp_scr[POOL_PREFIX - j:POOL_PREFIX - j + t, sl]
        r = acc / jnp.minimum(pos, float(win)) - cur
        outs.append(jnp.dot(r.astype(BF16), w_ref[g], preferred_element_type=F32))
    o_ref[0] = jnp.concatenate(outs, axis=1) * sc_ref[...]


def _pool_mix(pin, prefix, w_pool16, scale, pos0):
    b, t, w = pin.shape
    seq = pl.BlockSpec((1, t, w), lambda i: (i, 0, 0))
    return pl.pallas_call(
        functools.partial(_pool_mix_kernel, pos0=pos0),
        grid=(b,),
        in_specs=[seq, pl.BlockSpec((1, POOL_PREFIX, w), lambda i: (i, 0, 0)),
                  pl.BlockSpec(w_pool16.shape, lambda i: (0, 0, 0)), pl.BlockSpec((1, w), lambda i: (0, 0))],
        out_specs=seq,
        out_shape=jax.ShapeDtypeStruct((b, t, w), F32),
        scratch_shapes=[pltpu.VMEM((POOL_PREFIX + t, w), F32)],
        compiler_params=_params(("parallel",)),
        name="pool_mix",
    )(pin, prefix, w_pool16, scale.reshape(1, w))


def _mix_out_kernel(x_ref, a_ref, p_ref, wo_ref, g_ref, wq_ref, keys_ref, x1_ref, ht_ref, s1_ref, s2_ref):
    aw = a_ref.shape[1]
    mix = (jnp.dot(a_ref[...].astype(BF16), wo_ref[0:aw, :], preferred_element_type=F32)
           + jnp.dot(p_ref[...].astype(BF16), wo_ref[aw:, :], preferred_element_type=F32))
    x1 = x_ref[...] + mix
    x1_ref[...] = x1
    h = _rms(x1, g_ref[...])
    h16 = h.astype(BF16)
    ht_ref[...] = h.T.astype(BF16)
    qq = jnp.dot(h16, wq_ref[...], preferred_element_type=F32)
    half = keys_ref.shape[2]
    for hd in range(PEER_HEADS):
        q1 = qq[:, (2 * hd) * half:(2 * hd + 1) * half]
        q2 = qq[:, (2 * hd + 1) * half:(2 * hd + 2) * half]
        s1_ref[hd] = _dot_nt_3pass(keys_ref[0], q1)
        s2_ref[hd] = _dot_nt_3pass(keys_ref[1], q2)


def _mix_out(x, attn, pool, wo16, g, wq16, keys, tm):
    t, d = x.shape
    aw = attn.shape[1]
    nk = keys.shape[1]
    row = lambda wdt: pl.BlockSpec((tm, wdt), lambda i: (i, 0))
    full = lambda a: pl.BlockSpec(a.shape, lambda i: (0,) * a.ndim)
    sc = pl.BlockSpec((PEER_HEADS, nk, tm), lambda i: (0, 0, i))
    return pl.pallas_call(
        _mix_out_kernel,
        grid=(t // tm,),
        in_specs=[row(d), row(aw), row(aw), full(wo16), pl.BlockSpec((1, d), lambda i: (0, 0)), full(wq16), full(keys)],
        out_specs=[row(d), pl.BlockSpec((d, tm), lambda i: (0, i)), sc, sc],
        out_shape=[jax.ShapeDtypeStruct((t, d), F32), jax.ShapeDtypeStruct((d, t), BF16),
                   jax.ShapeDtypeStruct((PEER_HEADS, nk, t), F32), jax.ShapeDtypeStruct((PEER_HEADS, nk, t), F32)],
        compiler_params=_params(("parallel",)),
        name="mix_out",
    )(x, attn, pool, wo16, g.reshape(1, d), wq16, keys)


def _top_values(s, count):
    vals = []
    cur = s
    for _ in range(count):
        mx = jnp.max(cur, axis=0, keepdims=True)
        vals.append(mx)
        cur = jnp.where(cur == mx, NEG, cur)
    return vals


def _compare_exchange(xs, i, j, descending=True):
    hi, lo = jnp.maximum(xs[i], xs[j]), jnp.minimum(xs[i], xs[j])
    xs[i], xs[j] = (hi, lo) if descending else (lo, hi)


def _bitonic_merge(xs):
    n = len(xs)
    j = n // 2
    while j >= 1:
        for i in range(n):
            if i ^ j > i:
                _compare_exchange(xs, i, i ^ j)
        j //= 2


def _top_sorted(s, count):
    sub = s.shape[0] // count
    xs = [s[k * sub:(k + 1) * sub] for k in range(count)]
    k = 2
    while k <= count:
        j = k // 2
        while j >= 1:
            for i in range(count):
                if i ^ j > i:
                    _compare_exchange(xs, i, i ^ j, descending=(i & k) == 0)
            j //= 2
        k *= 2
    shift = sub // 2
    while shift >= 1:
        ys = [pltpu.roll(x, shift, 0) for x in xs]
        xs = [jnp.maximum(xs[i], ys[count - 1 - i]) for i in range(count)]
        _bitonic_merge(xs)
        shift //= 2
    return [x[0:1] for x in xs]


def _peer_gate_kernel(s1_ref, s2_ref, e1_ref, e2_ref, thr_ref):
    s1 = s1_ref[0]
    s2 = s2_ref[0]
    v1 = _top_sorted(s1, PEER_TOPK)
    v2 = _top_sorted(s2, PEER_TOPK)
    pairs = [(a, b) for a in range(PEER_TOPK) for b in range(PEER_TOPK // (a + 1))]
    pad = [jnp.full_like(v1[0], NEG)] * (-len(pairs) % 8)
    cand = jnp.concatenate([v1[a] + v2[b] for a, b in pairs] + pad, axis=0)
    tau = _top_values(cand, PEER_TOPK)[-1]
    cmax = v1[0] + v2[0]
    chosen = cand >= tau
    z = jnp.sum(jnp.where(chosen, jnp.exp(cand - cmax), 0.0), axis=0, keepdims=True)
    e1 = lambda s: (jnp.exp(s - v1[0]) / z).astype(BF16)
    e2 = lambda s: jnp.exp(s - v2[0]).astype(BF16)
    e1_ref[0] = jnp.where(s1 >= v1[-1], e1(s1).astype(F32), 0.0)
    e2_ref[0] = jnp.where(s2 >= v2[-1], e2(s2), jnp.zeros_like(s2, BF16))
    v2m = jnp.concatenate(v2, axis=0)
    e2m = e2(v2m)
    thr = None
    for a in range(PEER_TOPK):
        prod = (e1(jnp.broadcast_to(v1[a], v2m.shape)) * e2m).astype(F32)
        low = jnp.min(jnp.where(v1[a] + v2m >= tau, prod, 1.0), axis=0, keepdims=True)
        thr = low if thr is None else jnp.minimum(thr, low)
    thr_ref[0] = thr


def _peer_gate(s1t, s2t, tl):
    nh, nk, t = s1t.shape
    spec = pl.BlockSpec((1, nk, tl), lambda h, i: (h, 0, i))
    tspec = pl.BlockSpec((1, 1, tl), lambda h, i: (h, 0, i))
    return pl.pallas_call(
        _peer_gate_kernel,
        grid=(nh, t // tl),
        in_specs=[spec, spec],
        out_specs=[spec, spec, tspec],
        out_shape=[jax.ShapeDtypeStruct((nh, nk, t), F32), jax.ShapeDtypeStruct((nh, nk, t), BF16),
                   jax.ShapeDtypeStruct((nh, 1, t), F32)],
        compiler_params=_params(("parallel", "parallel")),
        name="peer_gate",
    )(s1t, s2t)


PEER_SUB_BLOCK = 256


def _peer_dense_kernel(ht_ref, x_ref, u_ref, vt_ref, e1_ref, e2_ref, thr_ref, gf_ref, y_ref,
                       a_scr, w_scr, acc_scr, *, final_norm):
    j = pl.program_id(1)
    tt = ht_ref.shape[1]
    nk = e2_ref.shape[1]
    eb = u_ref.shape[0]

    @pl.when(j == 0)
    def _():
        acc_scr[...] = jnp.zeros_like(acc_scr)

    ht = ht_ref[...]
    zero = jnp.zeros((nk, LANES), BF16)
    n_sub = eb // PEER_SUB_BLOCK

    def pre_activations(sb):
        sub = slice(sb * PEER_SUB_BLOCK, (sb + 1) * PEER_SUB_BLOCK)
        a_scr[sub, :] = jnp.dot(u_ref[sub, :], ht, preferred_element_type=F32)

    def gated_weights(sb):
        for c in range(sb * PEER_SUB_BLOCK // nk, (sb + 1) * PEER_SUB_BLOCK // nk):
            rows = slice(c * nk, (c + 1) * nk)
            for lt in range(tt // LANES):
                ls = slice(lt * LANES, (lt + 1) * LANES)
                gate = None
                for hd in range(PEER_HEADS):
                    e1 = jnp.broadcast_to(e1_ref[hd, c:c + 1, ls], (nk, LANES)).astype(BF16)
                    thr = jnp.broadcast_to(thr_ref[hd, :, ls], (nk, LANES)).astype(BF16)
                    p = e1 * e2_ref[hd, :, ls]
                    g = jnp.where(p >= thr, p, zero)
                    gate = g if gate is None else gate + g
                a = a_scr[rows, ls]
                gelu = 0.5 * a * (1.0 + lax.erf(a * (2.0 ** -0.5)))
                w_scr[rows, ls] = gate * gelu.astype(BF16)

    pre_activations(0)
    for sb in range(n_sub):
        if sb + 1 < n_sub:
            pre_activations(sb + 1)
        gated_weights(sb)
        sub = slice(sb * PEER_SUB_BLOCK, (sb + 1) * PEER_SUB_BLOCK)
        acc_scr[...] += jnp.dot(vt_ref[0, :, sub], w_scr[sub, :], preferred_element_type=F32)

    @pl.when(j == pl.num_programs(1) - 1)
    def _():
        y = x_ref[...] + acc_scr[...].T
        y_ref[...] = _rms(y, gf_ref[...]) if final_norm else y


PEER_EXPERT_BLOCK = 1024


def _peer_dense(ht16, x, u16, vt16, e1t, e2t, thr, g_final, final_norm, tt):
    t, d = x.shape
    e = u16.shape[0]
    eb = PEER_EXPERT_BLOCK
    nh, nk, _ = e1t.shape
    assert eb % PEER_SUB_BLOCK == 0 and PEER_SUB_BLOCK % nk == 0 and vt16.shape == (e // eb, d, eb)
    tok = lambda: pl.BlockSpec((tt, d), lambda i, j: (i, 0))
    return pl.pallas_call(
        functools.partial(_peer_dense_kernel, final_norm=final_norm),
        grid=(t // tt, e // eb),
        in_specs=[pl.BlockSpec((d, tt), lambda i, j: (0, i)), tok(),
                  pl.BlockSpec((eb, d), lambda i, j: (j, 0)),
                  pl.BlockSpec((1, d, eb), lambda i, j: (j, 0, 0)),
                  pl.BlockSpec((nh, eb // nk, tt), lambda i, j: (0, j, i)),
                  pl.BlockSpec((nh, nk, tt), lambda i, j: (0, 0, i)),
                  pl.BlockSpec((nh, 1, tt), lambda i, j: (0, 0, i)),
                  pl.BlockSpec((1, d), lambda i, j: (0, 0))],
        out_specs=tok(),
        out_shape=jax.ShapeDtypeStruct((t, d), F32),
        scratch_shapes=[pltpu.VMEM((eb, tt), F32), pltpu.VMEM((eb, tt), BF16), pltpu.VMEM((d, tt), F32)],
        compiler_params=_params(("parallel", "arbitrary")),
        name="peer_dense",
    )(ht16, x, u16, vt16, e1t, e2t, thr, g_final.reshape(1, d))


def _token_tile(t, want):
    return want if t % want == 0 else t


def _ffn(x, attn, pool, wo16, g_ffn, wq16, keys, u16, vt16, g_final, final_norm):
    t = x.shape[0]
    x1, ht16, s1t, s2t = _mix_out(x, attn, pool, wo16, g_ffn, wq16, keys, _token_tile(t, 256))
    e1t, e2t, thr = _peer_gate(s1t, s2t, _token_tile(t, 512))
    return _peer_dense(ht16, x1, u16, vt16, e1t, e2t, thr, g_final, final_norm, _token_tile(t, 512))


def kernel(x_prompt, x_sample, cache_k, cache_v, state_pool, page_table, norm_mix, w_in, w_pool, pool_scale,
           w_out, norm_ffn, peer_wq, peer_keys, peer_u, peer_v, norm_final):
    bp, sp, d = x_prompt.shape
    db, ts, _ = x_sample.shape
    depth = w_in.shape[0]
    aw = w_in.shape[2] // 4
    n_heads = aw // HEAD_DIM
    pool_buf = state_pool.shape[2]
    assert pool_buf == POOL_PREFIX - 1 and sp % MOBA_BLOCK == 0 and aw % LANES == 0
    n_pool, _, page, _, _ = cache_k.shape
    ck = cache_k.transpose(0, 1, 3, 4, 2).reshape(n_pool, depth, aw, page)
    cv = cache_v.transpose(0, 1, 3, 4, 2).reshape(n_pool, depth, aw, page)
    past_len = page_table.shape[1] * page
    ts_pad = -(-ts // 8) * 8

    y_p = x_prompt.reshape(bp * sp, d)
    y_s = x_sample.reshape(db * ts, d)
    kp, vp, pp, kss, vss, pss = [], [], [], [], [], []
    for l in range(depth):
        w_in16 = w_in[l].astype(BF16)
        w_pool16 = w_pool[l].astype(BF16)
        wo16 = w_out[l].astype(BF16)
        wq16 = peer_wq[l].astype(BF16)
        u16 = peer_u[l].astype(BF16)
        n_eb = peer_v.shape[1] // PEER_EXPERT_BLOCK
        vt16 = peer_v[l].astype(BF16).reshape(n_eb, PEER_EXPERT_BLOCK, d).transpose(0, 2, 1)
        q, k, v, pin = _norm_proj(y_p, norm_mix[l], w_in16, 512)
        q3, k3, v3, pin3 = (a.reshape(bp, sp, aw) for a in (q, k, v, pin))
        attn = _moba_prompt(q3, k3, v3)
        pool = _pool_mix(pin3, jnp.zeros((bp, POOL_PREFIX, aw), F32), w_pool16, pool_scale[l], 0)
        y_p = _ffn(y_p, attn.reshape(bp * sp, aw), pool.reshape(bp * sp, aw), wo16, norm_ffn[l], wq16,
                   peer_keys[l], u16, vt16, norm_final, l == depth - 1)
        kp.append(k3.reshape(bp, sp, n_heads, HEAD_DIM))
        vp.append(v3.reshape(bp, sp, n_heads, HEAD_DIM))
        pp.append(pin3[:, sp - pool_buf:])
        q, k, v, pin = _norm_proj(y_s, norm_mix[l], w_in16, db * ts)
        q3, k3, v3, pin3 = (a.reshape(db, ts, aw) for a in (q, k, v, pin))
        attn = _moba_sample(q3, k3, v3, ck, cv, page_table, l)
        prefix = jnp.concatenate([jnp.zeros((db, 1, aw), F32), state_pool[l]], axis=1)
        pin_pad = jnp.pad(pin3, ((0, 0), (0, ts_pad - ts), (0, 0)))
        pool = _pool_mix(pin_pad, prefix, w_pool16, pool_scale[l], past_len)[:, :ts]
        y_s = _ffn(y_s, attn.reshape(db * ts, aw), pool.reshape(db * ts, aw), wo16, norm_ffn[l], wq16,
                   peer_keys[l], u16, vt16, norm_final, l == depth - 1)
        kss.append(k3.reshape(db, ts, n_heads, HEAD_DIM))
        vss.append(v3.reshape(db, ts, n_heads, HEAD_DIM))
        pss.append(jnp.concatenate([state_pool[l], pin3], axis=1)[:, ts:])
    y_prompt = y_p.reshape(bp, sp, d)
    y_sample = y_s.reshape(db, ts, d)
    return (y_prompt, y_sample, jnp.stack(kp, axis=1), jnp.stack(vp, axis=1), jnp.stack(pp, axis=0),
            jnp.stack(kss, axis=1), jnp.stack(vss, axis=1), jnp.stack(pss, axis=0))
```

```python
import functools
import math

import jax
import jax.numpy as jnp
from jax import lax
from jax.experimental import pallas as pl
from jax.experimental.pallas import tpu as pltpu

F32 = jnp.float32
BF16 = jnp.bfloat16

HEAD_DIM = 64
MOBA_BLOCK = 256
MOBA_TOPK = 3
POOL_WINDOWS = (2, 4, 8, 16)
POOL_PREFIX = 16
PEER_HEADS = 8
PEER_NKEYS = 128
PEER_TOPK = 16
RMS_EPS = 1e-6
NEG = -1e30
LOG2E = 1.4426950408889634

LANES = 128
VMEM_LIMIT = 56 * 1024 * 1024

_NT = (((1,), (1,)), ((), ()))


def _params(semantics):
    return pltpu.CompilerParams(dimension_semantics=semantics, vmem_limit_bytes=VMEM_LIMIT)


def _rms(x, g):
    return x * lax.rsqrt(jnp.mean(x * x, axis=-1, keepdims=True) + RMS_EPS) * g


def _split_bf16(x):
    hi = x.astype(BF16)
    lo = (x - hi.astype(F32)).astype(BF16)
    return hi, lo


def _dot_nt_3pass(a, b):
    a_hi, a_lo = _split_bf16(a)
    b_hi, b_lo = _split_bf16(b)
    d = functools.partial(lax.dot_general, dimension_numbers=_NT, preferred_element_type=F32)
    return d(a_hi, b_hi) + (d(a_hi, b_lo) + d(a_lo, b_hi))


def _norm_proj_kernel(x_ref, g_ref, w_ref, q_ref, k_ref, v_ref, p_ref):
    h = _rms(x_ref[...], g_ref[...])
    proj = jnp.dot(h.astype(BF16), w_ref[...], preferred_element_type=F32)
    w = q_ref.shape[1]
    q_ref[...] = proj[:, 0 * w:1 * w]
    k_ref[...] = proj[:, 1 * w:2 * w]
    v_ref[...] = proj[:, 2 * w:3 * w]
    p_ref[...] = proj[:, 3 * w:4 * w]


def _norm_proj(x, g, w16, tm):
    t, d = x.shape
    w = w16.shape[1] // 4
    row = pl.BlockSpec((tm, d), lambda i: (i, 0))
    out = pl.BlockSpec((tm, w), lambda i: (i, 0))
    return pl.pallas_call(
        _norm_proj_kernel,
        grid=(t // tm,),
        in_specs=[row, pl.BlockSpec((1, d), lambda i: (0, 0)), pl.BlockSpec(w16.shape, lambda i: (0, 0))],
        out_specs=[out] * 4,
        out_shape=[jax.ShapeDtypeStruct((t, w), F32)] * 4,
        compiler_params=_params(("parallel",)),
        name="norm_proj",
    )(x, g.reshape(1, d), w16)


def _block_rank_select(gate, n_cand):
    lane = lax.broadcasted_iota(jnp.int32, gate.shape, 1)
    rank = jnp.zeros(gate.shape, F32)
    for m in range(n_cand):
        gm = gate[:, m:m + 1]
        beats = jnp.where(gm > gate, 1.0, jnp.where(gm == gate, jnp.where(lane > m, 1.0, 0.0), 0.0))
        rank = rank + beats
    return jnp.where(lane < n_cand, rank, float(MOBA_TOPK)) < float(MOBA_TOPK)


def _moba_prompt_kernel(q_ref, k_ref, v_ref, o_ref):
    s_len = q_ref.shape[1]
    nb = s_len // MOBA_BLOCK
    k = k_ref[0]
    k16 = k.astype(BF16)
    v16 = v_ref[0].astype(BF16)
    lane = lax.broadcasted_iota(jnp.int32, (1, LANES), 1)
    head_lanes = [lane < HEAD_DIM, lane >= HEAD_DIM]
    nbp = -(-nb // 8) * 8
    means = [jnp.mean(k[n * MOBA_BLOCK:(n + 1) * MOBA_BLOCK], axis=0, keepdims=True) for n in range(nb)]
    kbar = jnp.concatenate(means + [jnp.zeros((1, LANES), F32)] * (nbp - nb), axis=0)
    row = lax.broadcasted_iota(jnp.int32, (MOBA_BLOCK, MOBA_BLOCK), 0)
    col = lax.broadcasted_iota(jnp.int32, (MOBA_BLOCK, MOBA_BLOCK), 1)
    causal = row >= col
    eye = jnp.where(row == col, 1.0, 0.0).astype(BF16)
    blk = lax.broadcasted_iota(jnp.int32, (nbp, MOBA_BLOCK), 0)
    scale = HEAD_DIM ** -0.5
    for i in range(nb):
        q = q_ref[0, i * MOBA_BLOCK:(i + 1) * MOBA_BLOCK, :]
        n_keys = (i + 1) * MOBA_BLOCK
        outs = []
        for hh in range(2):
            qh = jnp.where(head_lanes[hh], q, 0.0)
            s = lax.dot_general((qh * scale).astype(BF16), k16[:n_keys], _NT, preferred_element_type=F32)
            pieces = []
            if i > 0:
                gate_t = _dot_nt_3pass(kbar, qh)
                rank = jnp.zeros(gate_t.shape, F32)
                for m in range(i):
                    gm = gate_t[m:m + 1]
                    rank = rank + jnp.where(gm > gate_t, 1.0,
                                            jnp.where(gm == gate_t, jnp.where(blk > m, 1.0, 0.0), 0.0))
                sel_t = jnp.where(blk < i, jnp.where(rank < float(MOBA_TOPK), 1.0, 0.0), 0.0)
                sel_t = jnp.concatenate([sel_t, jnp.zeros((LANES - nbp, MOBA_BLOCK), F32)], axis=0)
                sel = lax.dot_general(eye, sel_t.astype(BF16), _NT, preferred_element_type=F32)
                pieces = [jnp.where(sel[:, n:n + 1] > 0.5, s[:, n * MOBA_BLOCK:(n + 1) * MOBA_BLOCK], NEG)
                          for n in range(i)]
            pieces.append(jnp.where(causal, s[:, i * MOBA_BLOCK:], NEG))
            s = jnp.concatenate(pieces, axis=1)
            m = jnp.max(s, axis=-1, keepdims=True)
            p = jnp.exp(s - m)
            l = jnp.sum(p, axis=-1, keepdims=True)
            o = jnp.dot(p.astype(BF16), v16[:n_keys], preferred_element_type=F32)
            outs.append(o / l)
        o_ref[0, i * MOBA_BLOCK:(i + 1) * MOBA_BLOCK, :] = jnp.where(head_lanes[0], outs[0], outs[1])


def _moba_prompt(q, k, v):
    b, s, w = q.shape
    spec = pl.BlockSpec((1, s, LANES), lambda bi, pi: (bi, 0, pi))
    return pl.pallas_call(
        _moba_prompt_kernel,
        grid=(b, w // LANES),
        in_specs=[spec] * 3,
        out_specs=spec,
        out_shape=jax.ShapeDtypeStruct((b, s, w), F32),
        compiler_params=_params(("parallel", "parallel")),
        name="moba_prompt",
    )(q, k, v)


def _moba_sample_kernel(pt_ref, q_ref, kn_ref, vn_ref, *refs, blocks_per_step, pages_per_block):
    del pt_ref
    n_pg = blocks_per_step * pages_per_block
    kt_refs, vt_refs = refs[:n_pg], refs[n_pg:2 * n_pg]
    o_ref, m_scr, l_scr, g_scr, o_scr = refs[2 * n_pg:]
    step = pl.program_id(1)
    tq, w = q_ref.shape[1], q_ref.shape[2]
    n_heads = w // HEAD_DIM
    rows = tq * n_heads
    scale = HEAD_DIM ** -0.5

    head_of_lane = lax.broadcasted_iota(jnp.int32, (n_heads, w), 1) // HEAD_DIM
    head_mask = jnp.where(head_of_lane == lax.broadcasted_iota(jnp.int32, (n_heads, w), 0), 1.0, 0.0)
    q = q_ref[0]
    qbd = jnp.concatenate([jnp.broadcast_to(q[j:j + 1], (n_heads, w)) * head_mask for j in range(tq)], axis=0)

    @pl.when(step == 0)
    def _():
        m_scr[...] = jnp.zeros_like(m_scr)
        l_scr[...] = jnp.zeros_like(l_scr)
        g_scr[...] = jnp.zeros_like(g_scr)

    qs16 = (qbd * scale).astype(BF16)
    blk_lane = lax.broadcasted_iota(jnp.int32, (rows, LANES), 1)
    m_all, l_all, g_all = m_scr[...], l_scr[...], g_scr[...]
    for jj in range(blocks_per_step):
        n = step * blocks_per_step + jj
        pgs = range(jj * pages_per_block, (jj + 1) * pages_per_block)
        s = jnp.concatenate([jnp.dot(qs16, kt_refs[pg][0, 0].astype(BF16), preferred_element_type=F32)
                             for pg in pgs], axis=1)
        m_n = jnp.max(s, axis=-1, keepdims=True)
        p = jnp.exp(s - m_n)
        l_n = jnp.sum(p, axis=-1, keepdims=True)
        g_n = jnp.sum(s, axis=-1, keepdims=True)
        p16 = p.astype(BF16)
        page = kt_refs[0].shape[3]
        o_n = None
        for i, pg in enumerate(pgs):
            part = lax.dot_general(p16[:, i * page:(i + 1) * page], vt_refs[pg][0, 0].astype(BF16), _NT,
                                   preferred_element_type=F32)
            o_n = part if o_n is None else o_n + part
        o_scr[n] = o_n
        here = blk_lane == n
        m_all = jnp.where(here, m_n, m_all)
        l_all = jnp.where(here, l_n, l_all)
        g_all = jnp.where(here, g_n, g_all)
    m_scr[...] = m_all
    l_scr[...] = l_all
    g_scr[...] = g_all

    @pl.when(step == pl.num_programs(1) - 1)
    def _():
        n_past = o_scr.shape[0]
        sel = _block_rank_select(g_scr[...], n_past)
        m_all = m_scr[...]
        m_run = jnp.max(jnp.where(sel, m_all, NEG), axis=-1, keepdims=True)
        q_idx = lax.broadcasted_iota(jnp.int32, (rows, 1), 0) // n_heads
        kn = kn_ref[0]
        vn = vn_ref[0]
        s_own = []
        for j in range(tq):
            sj = jnp.sum(qbd * scale * kn[j:j + 1], axis=-1, keepdims=True)
            s_own.append(jnp.where(q_idx >= j, sj, NEG))
            m_run = jnp.maximum(m_run, s_own[j])
        wgt = jnp.where(sel, jnp.exp(m_all - m_run), 0.0)
        den = jnp.sum(wgt * l_scr[...], axis=-1, keepdims=True)
        num = jnp.zeros((rows, w), F32)
        for nn in range(n_past):
            num = num + wgt[:, nn:nn + 1] * o_scr[nn]
        for j in range(tq):
            e = jnp.where(q_idx >= j, jnp.exp(s_own[j] - m_run), 0.0)
            den = den + e
            num = num + e * vn[j:j + 1]
        acc = num / den
        out_rows = [jnp.sum(acc[j * n_heads:(j + 1) * n_heads] * head_mask, axis=0, keepdims=True)
                    for j in range(tq)]
        o_ref[0] = jnp.concatenate(out_rows, axis=0)


SAMPLE_BLOCKS_PER_STEP = 8


def _moba_sample(q, k_new, v_new, cache_kt, cache_vt, page_table, layer):
    db, tq, w = q.shape
    page = cache_kt.shape[3]
    n_pages = page_table.shape[1]
    ppb = MOBA_BLOCK // page
    assert MOBA_BLOCK == ppb * page and n_pages % (ppb * SAMPLE_BLOCKS_PER_STEP) == 0
    n_past = n_pages // ppb
    assert n_past <= LANES and tq <= MOBA_BLOCK
    rows = tq * (w // HEAD_DIM)
    n_pg = SAMPLE_BLOCKS_PER_STEP * ppb
    tok = pl.BlockSpec((1, tq, w), lambda b, s, pt: (b, 0, 0))

    def page_spec(which):
        return pl.BlockSpec((1, 1, w, page), lambda b, s, pt: (pt[b, s * n_pg + which], layer, 0, 0))

    pages = [page_spec(i) for i in range(n_pg)]
    grid_spec = pltpu.PrefetchScalarGridSpec(
        num_scalar_prefetch=1,
        grid=(db, n_past // SAMPLE_BLOCKS_PER_STEP),
        in_specs=[tok, tok, tok] + pages + pages,
        out_specs=tok,
        scratch_shapes=[pltpu.VMEM((rows, LANES), F32)] * 3 + [pltpu.VMEM((n_past, rows, w), F32)],
    )
    return pl.pallas_call(
        functools.partial(_moba_sample_kernel, blocks_per_step=SAMPLE_BLOCKS_PER_STEP, pages_per_block=ppb),
        grid_spec=grid_spec,
        out_shape=jax.ShapeDtypeStruct((db, tq, w), F32),
        compiler_params=_params(("parallel", "arbitrary")),
        name="moba_sample",
    )(page_table, q, k_new, v_new, *([cache_kt] * n_pg), *([cache_vt] * n_pg))


def _pool_mix_kernel(x_ref, pre_ref, w_ref, sc_ref, o_ref, xp_scr, *, pos0):
    t = x_ref.shape[1]
    gw = w_ref.shape[1]
    xp_scr[0:POOL_PREFIX, :] = pre_ref[0]
    xp_scr[POOL_PREFIX:POOL_PREFIX + t, :] = x_ref[0]
    pos = (lax.broadcasted_iota(jnp.int32, (t, 1), 0) + (pos0 + 1)).astype(F32)
    outs = []
    for g, win in enumerate(POOL_WINDOWS):
        sl = slice(g * gw, (g + 1) * gw)
        cur = xp_scr[POOL_PREFIX:POOL_PREFIX + t, sl]
        acc = cur
        for j in range(1, win):
            acc = acc + xp_scr[POOL_PREFIX - j:POOL_PREFIX - j + t, sl]
        r = acc / jnp.minimum(pos, float(win)) - cur
        outs.append(jnp.dot(r.astype(BF16), w_ref[g], preferred_element_type=F32))
    o_ref[0] = jnp.concatenate(outs, axis=1) * sc_ref[...]


def _pool_mix(pin, prefix, w_pool16, scale, pos0):
    b, t, w = pin.shape
    seq = pl.BlockSpec((1, t, w), lambda i: (i, 0, 0))
    return pl.pallas_call(
        functools.partial(_pool_mix_kernel, pos0=pos0),
        grid=(b,),
        in_specs=[seq, pl.BlockSpec((1, POOL_PREFIX, w), lambda i: (i, 0, 0)),
                  pl.BlockSpec(w_pool16.shape, lambda i: (0, 0, 0)), pl.BlockSpec((1, w), lambda i: (0, 0))],
        out_specs=seq,
        out_shape=jax.ShapeDtypeStruct((b, t, w), F32),
        scratch_shapes=[pltpu.VMEM((POOL_PREFIX + t, w), F32)],
        compiler_params=_params(("parallel",)),
        name="pool_mix",
    )(pin, prefix, w_pool16, scale.reshape(1, w))


def _mix_out_kernel(x_ref, a_ref, p_ref, wo_ref, g_ref, wq_ref, keys_ref, x1_ref, ht_ref, e1_ref, e2_ref, thr_ref):
    aw = a_ref.shape[1]
    mix = (jnp.dot(a_ref[...].astype(BF16), wo_ref[0:aw, :], preferred_element_type=F32)
           + jnp.dot(p_ref[...].astype(BF16), wo_ref[aw:, :], preferred_element_type=F32))
    x1 = x_ref[...] + mix
    x1_ref[...] = x1
    h = _rms(x1, g_ref[...])
    h16 = h.astype(BF16)
    ht_ref[...] = h.T.astype(BF16)
    qq = jnp.dot(h16, wq_ref[...], preferred_element_type=F32)
    half = keys_ref.shape[2]
    for hd in range(PEER_HEADS):
        q1 = qq[:, (2 * hd) * half:(2 * hd + 1) * half]
        q2 = qq[:, (2 * hd + 1) * half:(2 * hd + 2) * half]
        s1 = _dot_nt_3pass(keys_ref[0], q1)
        s2 = _dot_nt_3pass(keys_ref[1], q2)
        e1_ref[hd], e2_ref[hd], thr_ref[hd] = _gate_factors(s1, s2)


def _mix_out(x, attn, pool, wo16, g, wq16, keys, tm):
    t, d = x.shape
    aw = attn.shape[1]
    nk = keys.shape[1]
    row = lambda wdt: pl.BlockSpec((tm, wdt), lambda i: (i, 0))
    full = lambda a: pl.BlockSpec(a.shape, lambda i: (0,) * a.ndim)
    sc = pl.BlockSpec((PEER_HEADS, nk, tm), lambda i: (0, 0, i))
    return pl.pallas_call(
        _mix_out_kernel,
        grid=(t // tm,),
        in_specs=[row(d), row(aw), row(aw), full(wo16), pl.BlockSpec((1, d), lambda i: (0, 0)), full(wq16), full(keys)],
        out_specs=[row(d), pl.BlockSpec((d, tm), lambda i: (0, i)), sc, sc,
                   pl.BlockSpec((PEER_HEADS, 1, tm), lambda i: (0, 0, i))],
        out_shape=[jax.ShapeDtypeStruct((t, d), F32), jax.ShapeDtypeStruct((d, t), BF16),
                   jax.ShapeDtypeStruct((PEER_HEADS, nk, t), F32), jax.ShapeDtypeStruct((PEER_HEADS, nk, t), BF16),
                   jax.ShapeDtypeStruct((PEER_HEADS, 1, t), F32)],
        compiler_params=_params(("parallel",)),
        name="mix_out",
    )(x, attn, pool, wo16, g.reshape(1, d), wq16, keys)


def _top_values(s, count):
    vals = []
    cur = s
    for _ in range(count):
        mx = jnp.max(cur, axis=0, keepdims=True)
        vals.append(mx)
        cur = jnp.where(cur == mx, NEG, cur)
    return vals


def _compare_exchange(xs, i, j, descending=True):
    hi, lo = jnp.maximum(xs[i], xs[j]), jnp.minimum(xs[i], xs[j])
    xs[i], xs[j] = (hi, lo) if descending else (lo, hi)


def _bitonic_merge(xs):
    n = len(xs)
    j = n // 2
    while j >= 1:
        for i in range(n):
            if i ^ j > i:
                _compare_exchange(xs, i, i ^ j)
        j //= 2


def _top_sorted(s, count):
    sub = s.shape[0] // count
    xs = [s[k * sub:(k + 1) * sub] for k in range(count)]
    k = 2
    while k <= count:
        j = k // 2
        while j >= 1:
            for i in range(count):
                if i ^ j > i:
                    _compare_exchange(xs, i, i ^ j, descending=(i & k) == 0)
            j //= 2
        k *= 2
    shift = sub // 2
    while shift >= 1:
        ys = [pltpu.roll(x, shift, 0) for x in xs]
        xs = [jnp.maximum(xs[i], ys[count - 1 - i]) for i in range(count)]
        _bitonic_merge(xs)
        shift //= 2
    return [x[0:1] for x in xs]


def _gate_factors(s1, s2):
    v1 = _top_sorted(s1, PEER_TOPK)
    v2 = _top_sorted(s2, PEER_TOPK)
    pairs = [(a, b) for a in range(PEER_TOPK) for b in range(PEER_TOPK // (a + 1))]
    pad = [jnp.full_like(v1[0], NEG)] * (-len(pairs) % 8)
    cand = jnp.concatenate([v1[a] + v2[b] for a, b in pairs] + pad, axis=0)
    tau = _top_values(cand, PEER_TOPK)[-1]
    cmax = v1[0] + v2[0]
    chosen = cand >= tau
    z = jnp.sum(jnp.where(chosen, jnp.exp(cand - cmax), 0.0), axis=0, keepdims=True)
    e1 = lambda s: (jnp.exp(s - v1[0]) / z).astype(BF16)
    e2 = lambda s: jnp.exp(s - v2[0]).astype(BF16)
    e1_dense = jnp.where(s1 >= v1[-1], e1(s1).astype(F32), 0.0)
    e2_dense = jnp.where(s2 >= v2[-1], e2(s2), jnp.zeros_like(s2, BF16))
    v2m = jnp.concatenate(v2, axis=0)
    e2m = e2(v2m)
    thr = None
    for a in range(PEER_TOPK):
        prod = (e1(jnp.broadcast_to(v1[a], v2m.shape)) * e2m).astype(F32)
        low = jnp.min(jnp.where(v1[a] + v2m >= tau, prod, 1.0), axis=0, keepdims=True)
        thr = low if thr is None else jnp.minimum(thr, low)
    return e1_dense, e2_dense, thr


PEER_SUB_BLOCK = 256
PEER_PRE_BLOCK = 512


def _peer_dense_kernel(ht_ref, x_ref, u_ref, vt_ref, e1_ref, e2_ref, thr_ref, gf_ref, y_ref,
                       a_scr, w_scr, acc_scr, *, final_norm):
    j = pl.program_id(1)
    tt = ht_ref.shape[1]
    nk = e2_ref.shape[1]
    eb = u_ref.shape[0]

    @pl.when(j == 0)
    def _():
        acc_scr[...] = jnp.zeros_like(acc_scr)

    ht = ht_ref[...]
    zero = jnp.zeros((nk, LANES), BF16)
    n_sub = eb // PEER_SUB_BLOCK

    subs_per_pre = PEER_PRE_BLOCK // PEER_SUB_BLOCK

    def pre_activations(pb):
        blk = slice(pb * PEER_PRE_BLOCK, (pb + 1) * PEER_PRE_BLOCK)
        a_scr[blk, :] = jnp.dot(u_ref[blk, :], ht, preferred_element_type=F32)

    def gated_weights(sb):
        for c in range(sb * PEER_SUB_BLOCK // nk, (sb + 1) * PEER_SUB_BLOCK // nk):
            rows = slice(c * nk, (c + 1) * nk)
            for lt in range(tt // LANES):
                ls = slice(lt * LANES, (lt + 1) * LANES)
                gate = None
                for hd in range(PEER_HEADS):
                    e1 = jnp.broadcast_to(e1_ref[hd, c:c + 1, ls], (nk, LANES)).astype(BF16)
                    thr = jnp.broadcast_to(thr_ref[hd, :, ls], (nk, LANES)).astype(BF16)
                    p = e1 * e2_ref[hd, :, ls]
                    g = jnp.where(p >= thr, p, zero)
                    gate = g if gate is None else gate + g
                a = a_scr[rows, ls]
                gelu = 0.5 * a * (1.0 + lax.erf(a * (2.0 ** -0.5)))
                w_scr[rows, ls] = gate * gelu.astype(BF16)

    pre_activations(0)
    for sb in range(n_sub):
        if (sb + 1) % subs_per_pre == 0 and sb + 1 < n_sub:
            pre_activations((sb + 1) // subs_per_pre)
        gated_weights(sb)
        sub = slice(sb * PEER_SUB_BLOCK, (sb + 1) * PEER_SUB_BLOCK)
        acc_scr[...] += jnp.dot(vt_ref[0, :, sub], w_scr[sub, :], preferred_element_type=F32)

    @pl.when(j == pl.num_programs(1) - 1)
    def _():
        y = x_ref[...] + acc_scr[...].T
        y_ref[...] = _rms(y, gf_ref[...]) if final_norm else y


PEER_EXPERT_BLOCK = 1024


def _peer_dense(ht16, x, u16, vt16, e1t, e2t, thr, g_final, final_norm, tt):
    t, d = x.shape
    e = u16.shape[0]
    eb = PEER_EXPERT_BLOCK
    nh, nk, _ = e1t.shape
    assert eb % PEER_SUB_BLOCK == 0 and PEER_SUB_BLOCK % nk == 0 and vt16.shape == (e // eb, d, eb)
    tok = lambda: pl.BlockSpec((tt, d), lambda i, j: (i, 0))
    return pl.pallas_call(
        functools.partial(_peer_dense_kernel, final_norm=final_norm),
        grid=(t // tt, e // eb),
        in_specs=[pl.BlockSpec((d, tt), lambda i, j: (0, i)), tok(),
                  pl.BlockSpec((eb, d), lambda i, j: (j, 0)),
                  pl.BlockSpec((1, d, eb), lambda i, j: (j, 0, 0)),
                  pl.BlockSpec((nh, eb // nk, tt), lambda i, j: (0, j, i)),
                  pl.BlockSpec((nh, nk, tt), lambda i, j: (0, 0, i)),
                  pl.BlockSpec((nh, 1, tt), lambda i, j: (0, 0, i)),
                  pl.BlockSpec((1, d), lambda i, j: (0, 0))],
        out_specs=tok(),
        out_shape=jax.ShapeDtypeStruct((t, d), F32),
        scratch_shapes=[pltpu.VMEM((eb, tt), F32), pltpu.VMEM((eb, tt), BF16), pltpu.VMEM((d, tt), F32)],
        compiler_params=_params(("parallel", "arbitrary")),
        name="peer_dense",
    )(ht16, x, u16, vt16, e1t, e2t, thr, g_final.reshape(1, d))


def _token_tile(t, want):
    return want if t % want == 0 else t


def _ffn(x, attn, pool, wo16, g_ffn, wq16, keys, u16, vt16, g_final, final_norm):
    t = x.shape[0]
    x1, ht16, e1t, e2t, thr = _mix_out(x, attn, pool, wo16, g_ffn, wq16, keys, _token_tile(t, 256))
    return _peer_dense(ht16, x1, u16, vt16, e1t, e2t, thr, g_final, final_norm, _token_tile(t, 512))


def kernel(x_prompt, x_sample, cache_k, cache_v, state_pool, page_table, norm_mix, w_in, w_pool, pool_scale,
           w_out, norm_ffn, peer_wq, peer_keys, peer_u, peer_v, norm_final):
    bp, sp, d = x_prompt.shape
    db, ts, _ = x_sample.shape
    depth = w_in.shape[0]
    aw = w_in.shape[2] // 4
    n_heads = aw // HEAD_DIM
    pool_buf = state_pool.shape[2]
    assert pool_buf == POOL_PREFIX - 1 and sp % MOBA_BLOCK == 0 and aw % LANES == 0
    n_pool, _, page, _, _ = cache_k.shape
    ck = cache_k.transpose(0, 1, 3, 4, 2).reshape(n_pool, depth, aw, page)
    cv = cache_v.transpose(0, 1, 3, 4, 2).reshape(n_pool, depth, aw, page)
    past_len = page_table.shape[1] * page
    ts_pad = -(-ts // 8) * 8

    y_p = x_prompt.reshape(bp * sp, d)
    y_s = x_sample.reshape(db * ts, d)
    kp, vp, pp, kss, vss, pss = [], [], [], [], [], []
    for l in range(depth):
        w_in16 = w_in[l].astype(BF16)
        w_pool16 = w_pool[l].astype(BF16)
        wo16 = w_out[l].astype(BF16)
        wq16 = peer_wq[l].astype(BF16)
        u16 = peer_u[l].astype(BF16)
        n_eb = peer_v.shape[1] // PEER_EXPERT_BLOCK
        vt16 = peer_v[l].astype(BF16).reshape(n_eb, PEER_EXPERT_BLOCK, d).transpose(0, 2, 1)
        q, k, v, pin = _norm_proj(y_p, norm_mix[l], w_in16, 512)
        q3, k3, v3, pin3 = (a.reshape(bp, sp, aw) for a in (q, k, v, pin))
        attn = _moba_prompt(q3, k3, v3)
        pool = _pool_mix(pin3, jnp.zeros((bp, POOL_PREFIX, aw), F32), w_pool16, pool_scale[l], 0)
        y_p = _ffn(y_p, attn.reshape(bp * sp, aw), pool.reshape(bp * sp, aw), wo16, norm_ffn[l], wq16,
                   peer_keys[l], u16, vt16, norm_final, l == depth - 1)
        kp.append(k3.reshape(bp, sp, n_heads, HEAD_DIM))
        vp.append(v3.reshape(bp, sp, n_heads, HEAD_DIM))
        pp.append(pin3[:, sp - pool_buf:])
        q, k, v, pin = _norm_proj(y_s, norm_mix[l], w_in16, db * ts)
        q3, k3, v3, pin3 = (a.reshape(db, ts, aw) for a in (q, k, v, pin))
        attn = _moba_sample(q3, k3, v3, ck, cv, page_table, l)
        prefix = jnp.concatenate([jnp.zeros((db, 1, aw), F32), state_pool[l]], axis=1)
        pin_pad = jnp.pad(pin3, ((0, 0), (0, ts_pad - ts), (0, 0)))
        pool = _pool_mix(pin_pad, prefix, w_pool16, pool_scale[l], past_len)[:, :ts]
        y_s = _ffn(y_s, attn.reshape(db * ts, aw), pool.reshape(db * ts, aw), wo16, norm_ffn[l], wq16,
                   peer_keys[l], u16, vt16, norm_final, l == depth - 1)
        kss.append(k3.reshape(db, ts, n_heads, HEAD_DIM))
        vss.append(v3.reshape(db, ts, n_heads, HEAD_DIM))
        pss.append(jnp.concatenate([state_pool[l], pin3], axis=1)[:, ts:])
    y_prompt = y_p.reshape(bp, sp, d)
    y_sample = y_s.reshape(db, ts, d)
    return (y_prompt, y_sample, jnp.stack(kp, axis=1), jnp.stack(vp, axis=1), jnp.stack(pp, axis=0),
            jnp.stack(kss, axis=1), jnp.stack(vss, axis=1), jnp.stack(pss, axis=0))
```

```python
import functools
import math

import jax
import jax.numpy as jnp
from jax import lax
from jax.experimental import pallas as pl
from jax.experimental.pallas import tpu as pltpu

F32 = jnp.float32
BF16 = jnp.bfloat16

HEAD_DIM = 64
MOBA_BLOCK = 256
MOBA_TOPK = 3
POOL_WINDOWS = (2, 4, 8, 16)
POOL_PREFIX = 16
PEER_HEADS = 8
PEER_NKEYS = 128
PEER_TOPK = 16
RMS_EPS = 1e-6
NEG = -1e30
LOG2E = 1.4426950408889634

LANES = 128
VMEM_LIMIT = 56 * 1024 * 1024

_NT = (((1,), (1,)), ((), ()))


def _params(semantics):
    return pltpu.CompilerParams(dimension_semantics=semantics, vmem_limit_bytes=VMEM_LIMIT)


def _rms(x, g):
    return x * lax.rsqrt(jnp.mean(x * x, axis=-1, keepdims=True) + RMS_EPS) * g


def _split_bf16(x):
    hi = x.astype(BF16)
    lo = (x - hi.astype(F32)).astype(BF16)
    return hi, lo


def _dot_nt_3pass(a, b):
    a_hi, a_lo = _split_bf16(a)
    b_hi, b_lo = _split_bf16(b)
    d = functools.partial(lax.dot_general, dimension_numbers=_NT, preferred_element_type=F32)
    return d(a_hi, b_hi) + (d(a_hi, b_lo) + d(a_lo, b_hi))


def _norm_proj_kernel(x_ref, g_ref, w_ref, q_ref, k_ref, v_ref, p_ref):
    h = _rms(x_ref[...], g_ref[...])
    proj = jnp.dot(h.astype(BF16), w_ref[...], preferred_element_type=F32)
    w = q_ref.shape[1]
    q_ref[...] = proj[:, 0 * w:1 * w]
    k_ref[...] = proj[:, 1 * w:2 * w]
    v_ref[...] = proj[:, 2 * w:3 * w]
    p_ref[...] = proj[:, 3 * w:4 * w]


def _norm_proj(x, g, w16, tm):
    t, d = x.shape
    w = w16.shape[1] // 4
    row = pl.BlockSpec((tm, d), lambda i: (i, 0))
    out = pl.BlockSpec((tm, w), lambda i: (i, 0))
    return pl.pallas_call(
        _norm_proj_kernel,
        grid=(t // tm,),
        in_specs=[row, pl.BlockSpec((1, d), lambda i: (0, 0)), pl.BlockSpec(w16.shape, lambda i: (0, 0))],
        out_specs=[out] * 4,
        out_shape=[jax.ShapeDtypeStruct((t, w), F32)] * 4,
        compiler_params=_params(("parallel",)),
        name="norm_proj",
    )(x, g.reshape(1, d), w16)


def _block_rank_select(gate, n_cand):
    lane = lax.broadcasted_iota(jnp.int32, gate.shape, 1)
    rank = jnp.zeros(gate.shape, F32)
    for m in range(n_cand):
        gm = gate[:, m:m + 1]
        beats = jnp.where(gm > gate, 1.0, jnp.where(gm == gate, jnp.where(lane > m, 1.0, 0.0), 0.0))
        rank = rank + beats
    return jnp.where(lane < n_cand, rank, float(MOBA_TOPK)) < float(MOBA_TOPK)


def _moba_prompt_kernel(q_ref, k_ref, v_ref, o_ref):
    s_len = q_ref.shape[1]
    nb = s_len // MOBA_BLOCK
    k = k_ref[0]
    k16 = k.astype(BF16)
    v16 = v_ref[0].astype(BF16)
    lane = lax.broadcasted_iota(jnp.int32, (1, LANES), 1)
    head_lanes = [lane < HEAD_DIM, lane >= HEAD_DIM]
    means = [jnp.mean(k[n * MOBA_BLOCK:(n + 1) * MOBA_BLOCK], axis=0, keepdims=True) for n in range(nb)]
    kbar = jnp.concatenate(means + [jnp.zeros((LANES - nb, LANES), F32)], axis=0)
    row = lax.broadcasted_iota(jnp.int32, (MOBA_BLOCK, MOBA_BLOCK), 0)
    col = lax.broadcasted_iota(jnp.int32, (MOBA_BLOCK, MOBA_BLOCK), 1)
    causal = row >= col
    scale = HEAD_DIM ** -0.5
    for i in range(nb):
        q = q_ref[0, i * MOBA_BLOCK:(i + 1) * MOBA_BLOCK, :]
        n_keys = (i + 1) * MOBA_BLOCK
        outs = []
        for hh in range(2):
            qh = jnp.where(head_lanes[hh], q, 0.0)
            s = lax.dot_general((qh * scale).astype(BF16), k16[:n_keys], _NT, preferred_element_type=F32)
            pieces = []
            if i > 0:
                gate = _dot_nt_3pass(qh, kbar)
                sel = jnp.where(_block_rank_select(gate, i), 1.0, 0.0)
                pieces = [jnp.where(sel[:, n:n + 1] > 0.5, s[:, n * MOBA_BLOCK:(n + 1) * MOBA_BLOCK], NEG)
                          for n in range(i)]
            pieces.append(jnp.where(causal, s[:, i * MOBA_BLOCK:], NEG))
            s = jnp.concatenate(pieces, axis=1)
            m = jnp.max(s, axis=-1, keepdims=True)
            p = jnp.exp(s - m)
            l = jnp.sum(p, axis=-1, keepdims=True)
            o = jnp.dot(p.astype(BF16), v16[:n_keys], preferred_element_type=F32)
            outs.append(o / l)
        o_ref[0, i * MOBA_BLOCK:(i + 1) * MOBA_BLOCK, :] = jnp.where(head_lanes[0], outs[0], outs[1])


def _moba_prompt(q, k, v):
    b, s, w = q.shape
    spec = pl.BlockSpec((1, s, LANES), lambda bi, pi: (bi, 0, pi))
    return pl.pallas_call(
        _moba_prompt_kernel,
        grid=(b, w // LANES),
        in_specs=[spec] * 3,
        out_specs=spec,
        out_shape=jax.ShapeDtypeStruct((b, s, w), F32),
        compiler_params=_params(("parallel", "parallel")),
        name="moba_prompt",
    )(q, k, v)


def _moba_sample_kernel(pt_ref, q_ref, kn_ref, vn_ref, *refs, blocks_per_step, pages_per_block):
    del pt_ref
    n_pg = blocks_per_step * pages_per_block
    kt_refs, vt_refs = refs[:n_pg], refs[n_pg:2 * n_pg]
    o_ref, m_scr, l_scr, g_scr, o_scr = refs[2 * n_pg:]
    step = pl.program_id(1)
    tq, w = q_ref.shape[1], q_ref.shape[2]
    n_heads = w // HEAD_DIM
    rows = tq * n_heads
    scale = HEAD_DIM ** -0.5

    head_of_lane = lax.broadcasted_iota(jnp.int32, (n_heads, w), 1) // HEAD_DIM
    head_mask = jnp.where(head_of_lane == lax.broadcasted_iota(jnp.int32, (n_heads, w), 0), 1.0, 0.0)
    q = q_ref[0]
    qbd = jnp.concatenate([jnp.broadcast_to(q[j:j + 1], (n_heads, w)) * head_mask for j in range(tq)], axis=0)

    @pl.when(step == 0)
    def _():
        m_scr[...] = jnp.zeros_like(m_scr)
        l_scr[...] = jnp.zeros_like(l_scr)
        g_scr[...] = jnp.zeros_like(g_scr)

    qs16 = (qbd * scale).astype(BF16)
    blk_lane = lax.broadcasted_iota(jnp.int32, (rows, LANES), 1)
    m_all, l_all, g_all = m_scr[...], l_scr[...], g_scr[...]
    for jj in range(blocks_per_step):
        n = step * blocks_per_step + jj
        pgs = range(jj * pages_per_block, (jj + 1) * pages_per_block)
        s = jnp.concatenate([jnp.dot(qs16, kt_refs[pg][0, 0].astype(BF16), preferred_element_type=F32)
                             for pg in pgs], axis=1)
        m_n = jnp.max(s, axis=-1, keepdims=True)
        p = jnp.exp(s - m_n)
        l_n = jnp.sum(p, axis=-1, keepdims=True)
        g_n = jnp.sum(s, axis=-1, keepdims=True)
        p16 = p.astype(BF16)
        page = kt_refs[0].shape[3]
        o_n = None
        for i, pg in enumerate(pgs):
            part = lax.dot_general(p16[:, i * page:(i + 1) * page], vt_refs[pg][0, 0].astype(BF16), _NT,
                                   preferred_element_type=F32)
            o_n = part if o_n is None else o_n + part
        o_scr[n] = o_n
        here = blk_lane == n
        m_all = jnp.where(here, m_n, m_all)
        l_all = jnp.where(here, l_n, l_all)
        g_all = jnp.where(here, g_n, g_all)
    m_scr[...] = m_all
    l_scr[...] = l_all
    g_scr[...] = g_all

    @pl.when(step == pl.num_programs(1) - 1)
    def _():
        n_past = o_scr.shape[0]
        sel = _block_rank_select(g_scr[...], n_past)
        m_all = m_scr[...]
        m_run = jnp.max(jnp.where(sel, m_all, NEG), axis=-1, keepdims=True)
        q_idx = lax.broadcasted_iota(jnp.int32, (rows, 1), 0) // n_heads
        kn = kn_ref[0]
        vn = vn_ref[0]
        s_own = []
        for j in range(tq):
            sj = jnp.sum(qbd * scale * kn[j:j + 1], axis=-1, keepdims=True)
            s_own.append(jnp.where(q_idx >= j, sj, NEG))
            m_run = jnp.maximum(m_run, s_own[j])
        wgt = jnp.where(sel, jnp.exp(m_all - m_run), 0.0)
        den = jnp.sum(wgt * l_scr[...], axis=-1, keepdims=True)
        num = jnp.zeros((rows, w), F32)
        for nn in range(n_past):
            num = num + wgt[:, nn:nn + 1] * o_scr[nn]
        for j in range(tq):
            e = jnp.where(q_idx >= j, jnp.exp(s_own[j] - m_run), 0.0)
            den = den + e
            num = num + e * vn[j:j + 1]
        acc = num / den
        out_rows = [jnp.sum(acc[j * n_heads:(j + 1) * n_heads] * head_mask, axis=0, keepdims=True)
                    for j in range(tq)]
        o_ref[0] = jnp.concatenate(out_rows, axis=0)


SAMPLE_BLOCKS_PER_STEP = 8


def _moba_sample(q, k_new, v_new, cache_kt, cache_vt, page_table, layer):
    db, tq, w = q.shape
    page = cache_kt.shape[3]
    n_pages = page_table.shape[1]
    ppb = MOBA_BLOCK // page
    assert MOBA_BLOCK == ppb * page and n_pages % (ppb * SAMPLE_BLOCKS_PER_STEP) == 0
    n_past = n_pages // ppb
    assert n_past <= LANES and tq <= MOBA_BLOCK
    rows = tq * (w // HEAD_DIM)
    n_pg = SAMPLE_BLOCKS_PER_STEP * ppb
    tok = pl.BlockSpec((1, tq, w), lambda b, s, pt: (b, 0, 0))

    def page_spec(which):
        return pl.BlockSpec((1, 1, w, page), lambda b, s, pt: (pt[b, s * n_pg + which], layer, 0, 0))

    pages = [page_spec(i) for i in range(n_pg)]
    grid_spec = pltpu.PrefetchScalarGridSpec(
        num_scalar_prefetch=1,
        grid=(db, n_past // SAMPLE_BLOCKS_PER_STEP),
        in_specs=[tok, tok, tok] + pages + pages,
        out_specs=tok,
        scratch_shapes=[pltpu.VMEM((rows, LANES), F32)] * 3 + [pltpu.VMEM((n_past, rows, w), F32)],
    )
    return pl.pallas_call(
        functools.partial(_moba_sample_kernel, blocks_per_step=SAMPLE_BLOCKS_PER_STEP, pages_per_block=ppb),
        grid_spec=grid_spec,
        out_shape=jax.ShapeDtypeStruct((db, tq, w), F32),
        compiler_params=_params(("parallel", "arbitrary")),
        name="moba_sample",
    )(page_table, q, k_new, v_new, *([cache_kt] * n_pg), *([cache_vt] * n_pg))


def _pool_mix_kernel(x_ref, pre_ref, w_ref, sc_ref, o_ref, xp_scr, *, pos0):
    t = x_ref.shape[1]
    gw = w_ref.shape[1]
    xp_scr[0:POOL_PREFIX, :] = pre_ref[0]
    xp_scr[POOL_PREFIX:POOL_PREFIX + t, :] = x_ref[0]
    pos = (lax.broadcasted_iota(jnp.int32, (t, 1), 0) + (pos0 + 1)).astype(F32)
    outs = []
    for g, win in enumerate(POOL_WINDOWS):
        sl = slice(g * gw, (g + 1) * gw)
        cur = xp_scr[POOL_PREFIX:POOL_PREFIX + t, sl]
        acc = cur
        for j in range(1, win):
            acc = acc + xp_scr[POOL_PREFIX - j:POOL_PREFIX - j + t, sl]
        r = acc / jnp.minimum(pos, float(win)) - cur
        outs.append(jnp.dot(r.astype(BF16), w_ref[g], preferred_element_type=F32))
    o_ref[0] = jnp.concatenate(outs, axis=1) * sc_ref[...]


def _pool_mix(pin, prefix, w_pool16, scale, pos0):
    b, t, w = pin.shape
    seq = pl.BlockSpec((1, t, w), lambda i: (i, 0, 0))
    return pl.pallas_call(
        functools.partial(_pool_mix_kernel, pos0=pos0),
        grid=(b,),
        in_specs=[seq, pl.BlockSpec((1, POOL_PREFIX, w), lambda i: (i, 0, 0)),
                  pl.BlockSpec(w_pool16.shape, lambda i: (0, 0, 0)), pl.BlockSpec((1, w), lambda i: (0, 0))],
        out_specs=seq,
        out_shape=jax.ShapeDtypeStruct((b, t, w), F32),
        scratch_shapes=[pltpu.VMEM((POOL_PREFIX + t, w), F32)],
        compiler_params=_params(("parallel",)),
        name="pool_mix",
    )(pin, prefix, w_pool16, scale.reshape(1, w))


def _mix_out_kernel(x_ref, a_ref, p_ref, wo_ref, g_ref, wq_ref, keys_ref, x1_ref, ht_ref, e1_ref, e2_ref, thr_ref):
    aw = a_ref.shape[1]
    mix = (jnp.dot(a_ref[...].astype(BF16), wo_ref[0:aw, :], preferred_element_type=F32)
           + jnp.dot(p_ref[...].astype(BF16), wo_ref[aw:, :], preferred_element_type=F32))
    x1 = x_ref[...] + mix
    x1_ref[...] = x1
    h = _rms(x1, g_ref[...])
    h16 = h.astype(BF16)
    ht_ref[...] = h.T.astype(BF16)
    qq = jnp.dot(h16, wq_ref[...], preferred_element_type=F32)
    half = keys_ref.shape[2]
    for hd in range(PEER_HEADS):
        q1 = qq[:, (2 * hd) * half:(2 * hd + 1) * half]
        q2 = qq[:, (2 * hd + 1) * half:(2 * hd + 2) * half]
        s1 = _dot_nt_3pass(keys_ref[0], q1)
        s2 = _dot_nt_3pass(keys_ref[1], q2)
        e1_ref[hd], e2_ref[hd], thr_ref[hd] = _gate_factors(s1, s2)


def _mix_out(x, attn, pool, wo16, g, wq16, keys, tm):
    t, d = x.shape
    aw = attn.shape[1]
    nk = keys.shape[1]
    row = lambda wdt: pl.BlockSpec((tm, wdt), lambda i: (i, 0))
    full = lambda a: pl.BlockSpec(a.shape, lambda i: (0,) * a.ndim)
    sc = pl.BlockSpec((PEER_HEADS, nk, tm), lambda i: (0, 0, i))
    return pl.pallas_call(
        _mix_out_kernel,
        grid=(t // tm,),
        in_specs=[row(d), row(aw), row(aw), full(wo16), pl.BlockSpec((1, d), lambda i: (0, 0)), full(wq16), full(keys)],
        out_specs=[row(d), pl.BlockSpec((d, tm), lambda i: (0, i)), sc, sc,
                   pl.BlockSpec((PEER_HEADS, 1, tm), lambda i: (0, 0, i))],
        out_shape=[jax.ShapeDtypeStruct((t, d), F32), jax.ShapeDtypeStruct((d, t), BF16),
                   jax.ShapeDtypeStruct((PEER_HEADS, nk, t), F32), jax.ShapeDtypeStruct((PEER_HEADS, nk, t), BF16),
                   jax.ShapeDtypeStruct((PEER_HEADS, 1, t), F32)],
        compiler_params=_params(("parallel",)),
        name="mix_out",
    )(x, attn, pool, wo16, g.reshape(1, d), wq16, keys)


def _top_values(s, count):
    vals = []
    cur = s
    for _ in range(count):
        mx = jnp.max(cur, axis=0, keepdims=True)
        vals.append(mx)
        cur = jnp.where(cur == mx, NEG, cur)
    return vals


def _compare_exchange(xs, i, j, descending=True):
    hi, lo = jnp.maximum(xs[i], xs[j]), jnp.minimum(xs[i], xs[j])
    xs[i], xs[j] = (hi, lo) if descending else (lo, hi)


def _bitonic_merge(xs):
    n = len(xs)
    j = n // 2
    while j >= 1:
        for i in range(n):
            if i ^ j > i:
                _compare_exchange(xs, i, i ^ j)
        j //= 2


def _top_sorted(s, count):
    sub = s.shape[0] // count
    xs = [s[k * sub:(k + 1) * sub] for k in range(count)]
    k = 2
    while k <= count:
        j = k // 2
        while j >= 1:
            for i in range(count):
                if i ^ j > i:
                    _compare_exchange(xs, i, i ^ j, descending=(i & k) == 0)
            j //= 2
        k *= 2
    shift = sub // 2
    while shift >= 1:
        ys = [pltpu.roll(x, shift, 0) for x in xs]
        xs = [jnp.maximum(xs[i], ys[count - 1 - i]) for i in range(count)]
        _bitonic_merge(xs)
        shift //= 2
    return [x[0:1] for x in xs]


def _gate_factors(s1, s2):
    v1 = _top_sorted(s1, PEER_TOPK)
    v2 = _top_sorted(s2, PEER_TOPK)
    pairs = [(a, b) for a in range(PEER_TOPK) for b in range(PEER_TOPK // (a + 1))]
    pad = [jnp.full_like(v1[0], NEG)] * (-len(pairs) % 8)
    cand = jnp.concatenate([v1[a] + v2[b] for a, b in pairs] + pad, axis=0)
    tau = _top_values(cand, PEER_TOPK)[-1]
    cmax = v1[0] + v2[0]
    chosen = cand >= tau
    z = jnp.sum(jnp.where(chosen, jnp.exp(cand - cmax), 0.0), axis=0, keepdims=True)
    e1 = lambda s: (jnp.exp(s - v1[0]) / z).astype(BF16)
    e2 = lambda s: jnp.exp(s - v2[0]).astype(BF16)
    e1_dense = jnp.where(s1 >= v1[-1], e1(s1).astype(F32), 0.0)
    e2_dense = jnp.where(s2 >= v2[-1], e2(s2), jnp.zeros_like(s2, BF16))
    v2m = jnp.concatenate(v2, axis=0)
    e2m = e2(v2m)
    thr = None
    for a in range(PEER_TOPK):
        prod = (e1(jnp.broadcast_to(v1[a], v2m.shape)) * e2m).astype(F32)
        low = jnp.min(jnp.where(v1[a] + v2m >= tau, prod, 1.0), axis=0, keepdims=True)
        thr = low if thr is None else jnp.minimum(thr, low)
    return e1_dense, e2_dense, thr


PEER_SUB_BLOCK = 256
PEER_PRE_BLOCK = 512


def _peer_dense_kernel(ht_ref, x_ref, u_ref, vt_ref, e1_ref, e2_ref, thr_ref, gf_ref, y_ref,
                       a_scr, w_scr, acc_scr, *, final_norm):
    j = pl.program_id(1)
    tt = ht_ref.shape[1]
    nk = e2_ref.shape[1]
    eb = u_ref.shape[0]

    @pl.when(j == 0)
    def _():
        acc_scr[...] = jnp.zeros_like(acc_scr)

    ht = ht_ref[...]
    zero = jnp.zeros((nk, LANES), BF16)
    n_sub = eb // PEER_SUB_BLOCK

    subs_per_pre = PEER_PRE_BLOCK // PEER_SUB_BLOCK

    def pre_activations(pb):
        blk = slice(pb * PEER_PRE_BLOCK, (pb + 1) * PEER_PRE_BLOCK)
        a_scr[blk, :] = jnp.dot(u_ref[blk, :], ht, preferred_element_type=F32)

    def gated_weights(sb):
        for c in range(sb * PEER_SUB_BLOCK // nk, (sb + 1) * PEER_SUB_BLOCK // nk):
            rows = slice(c * nk, (c + 1) * nk)
            for lt in range(tt // LANES):
                ls = slice(lt * LANES, (lt + 1) * LANES)
                gate = None
                for hd in range(PEER_HEADS):
                    e1 = jnp.broadcast_to(e1_ref[hd, c:c + 1, ls], (nk, LANES)).astype(BF16)
                    thr = jnp.broadcast_to(thr_ref[hd, :, ls], (nk, LANES)).astype(BF16)
                    p = e1 * e2_ref[hd, :, ls]
                    g = jnp.where(p >= thr, p, zero)
                    gate = g if gate is None else gate + g
                a = a_scr[rows, ls]
                gelu = 0.5 * a * (1.0 + lax.erf(a * (2.0 ** -0.5)))
                w_scr[rows, ls] = gate * gelu.astype(BF16)

    pre_activations(0)
    for sb in range(n_sub):
        if (sb + 1) % subs_per_pre == 0 and sb + 1 < n_sub:
            pre_activations((sb + 1) // subs_per_pre)
        gated_weights(sb)
        sub = slice(sb * PEER_SUB_BLOCK, (sb + 1) * PEER_SUB_BLOCK)
        acc_scr[...] += jnp.dot(vt_ref[0, :, sub], w_scr[sub, :], preferred_element_type=F32)

    @pl.when(j == pl.num_programs(1) - 1)
    def _():
        y = x_ref[...] + acc_scr[...].T
        y_ref[...] = _rms(y, gf_ref[...]) if final_norm else y


PEER_EXPERT_BLOCK = 1024


def _peer_dense(ht16, x, u16, vt16, e1t, e2t, thr, g_final, final_norm, tt):
    t, d = x.shape
    e = u16.shape[0]
    eb = PEER_EXPERT_BLOCK
    nh, nk, _ = e1t.shape
    assert eb % PEER_SUB_BLOCK == 0 and PEER_SUB_BLOCK % nk == 0 and vt16.shape == (e // eb, d, eb)
    tok = lambda: pl.BlockSpec((tt, d), lambda i, j: (i, 0))
    return pl.pallas_call(
        functools.partial(_peer_dense_kernel, final_norm=final_norm),
        grid=(t // tt, e // eb),
        in_specs=[pl.BlockSpec((d, tt), lambda i, j: (0, i)), tok(),
                  pl.BlockSpec((eb, d), lambda i, j: (j, 0)),
                  pl.BlockSpec((1, d, eb), lambda i, j: (j, 0, 0)),
                  pl.BlockSpec((nh, eb // nk, tt), lambda i, j: (0, j, i)),
                  pl.BlockSpec((nh, nk, tt), lambda i, j: (0, 0, i)),
                  pl.BlockSpec((nh, 1, tt), lambda i, j: (0, 0, i)),
                  pl.BlockSpec((1, d), lambda i, j: (0, 0))],
        out_specs=tok(),
        out_shape=jax.ShapeDtypeStruct((t, d), F32),
        scratch_shapes=[pltpu.VMEM((eb, tt), F32), pltpu.VMEM((eb, tt), BF16), pltpu.VMEM((d, tt), F32)],
        compiler_params=_params(("parallel", "arbitrary")),
        name="peer_dense",
    )(ht16, x, u16, vt16, e1t, e2t, thr, g_final.reshape(1, d))


def _token_tile(t, want):
    return want if t % want == 0 else t


def _ffn(x, attn, pool, wo16, g_ffn, wq16, keys, u16, vt16, g_final, final_norm):
    t = x.shape[0]
    x1, ht16, e1t, e2t, thr = _mix_out(x, attn, pool, wo16, g_ffn, wq16, keys, _token_tile(t, 256))
    return _peer_dense(ht16, x1, u16, vt16, e1t, e2t, thr, g_final, final_norm, _token_tile(t, 1024))


def kernel(x_prompt, x_sample, cache_k, cache_v, state_pool, page_table, norm_mix, w_in, w_pool, pool_scale,
           w_out, norm_ffn, peer_wq, peer_keys, peer_u, peer_v, norm_final):
    bp, sp, d = x_prompt.shape
    db, ts, _ = x_sample.shape
    depth = w_in.shape[0]
    aw = w_in.shape[2] // 4
    n_heads = aw // HEAD_DIM
    pool_buf = state_pool.shape[2]
    assert pool_buf == POOL_PREFIX - 1 and sp % MOBA_BLOCK == 0 and aw % LANES == 0
    n_pool, _, page, _, _ = cache_k.shape
    ck = cache_k.transpose(0, 1, 3, 4, 2).reshape(n_pool, depth, aw, page)
    cv = cache_v.transpose(0, 1, 3, 4, 2).reshape(n_pool, depth, aw, page)
    past_len = page_table.shape[1] * page
    ts_pad = -(-ts // 8) * 8

    y_p = x_prompt.reshape(bp * sp, d)
    y_s = x_sample.reshape(db * ts, d)
    kp, vp, pp, kss, vss, pss = [], [], [], [], [], []
    for l in range(depth):
        w_in16 = w_in[l].astype(BF16)
        w_pool16 = w_pool[l].astype(BF16)
        wo16 = w_out[l].astype(BF16)
        wq16 = peer_wq[l].astype(BF16)
        u16 = peer_u[l].astype(BF16)
        n_eb = peer_v.shape[1] // PEER_EXPERT_BLOCK
        vt16 = peer_v[l].astype(BF16).reshape(n_eb, PEER_EXPERT_BLOCK, d).transpose(0, 2, 1)
        q, k, v, pin = _norm_proj(y_p, norm_mix[l], w_in16, 512)
        q3, k3, v3, pin3 = (a.reshape(bp, sp, aw) for a in (q, k, v, pin))
        attn = _moba_prompt(q3, k3, v3)
        pool = _pool_mix(pin3, jnp.zeros((bp, POOL_PREFIX, aw), F32), w_pool16, pool_scale[l], 0)
        y_p = _ffn(y_p, attn.reshape(bp * sp, aw), pool.reshape(bp * sp, aw), wo16, norm_ffn[l], wq16,
                   peer_keys[l], u16, vt16, norm_final, l == depth - 1)
        kp.append(k3.reshape(bp, sp, n_heads, HEAD_DIM))
        vp.append(v3.reshape(bp, sp, n_heads, HEAD_DIM))
        pp.append(pin3[:, sp - pool_buf:])
        q, k, v, pin = _norm_proj(y_s, norm_mix[l], w_in16, db * ts)
        q3, k3, v3, pin3 = (a.reshape(db, ts, aw) for a in (q, k, v, pin))
        attn = _moba_sample(q3, k3, v3, ck, cv, page_table, l)
        prefix = jnp.concatenate([jnp.zeros((db, 1, aw), F32), state_pool[l]], axis=1)
        pin_pad = jnp.pad(pin3, ((0, 0), (0, ts_pad - ts), (0, 0)))
        pool = _pool_mix(pin_pad, prefix, w_pool16, pool_scale[l], past_len)[:, :ts]
        y_s = _ffn(y_s, attn.reshape(db * ts, aw), pool.reshape(db * ts, aw), wo16, norm_ffn[l], wq16,
                   peer_keys[l], u16, vt16, norm_final, l == depth - 1)
        kss.append(k3.reshape(db, ts, n_heads, HEAD_DIM))
        vss.append(v3.reshape(db, ts, n_heads, HEAD_DIM))
        pss.append(jnp.concatenate([state_pool[l], pin3], axis=1)[:, ts:])
    y_prompt = y_p.reshape(bp, sp, d)
    y_sample = y_s.reshape(db, ts, d)
    return (y_prompt, y_sample, jnp.stack(kp, axis=1), jnp.stack(vp, axis=1), jnp.stack(pp, axis=0),
            jnp.stack(kss, axis=1), jnp.stack(vss, axis=1), jnp.stack(pss, axis=0))
```

```python
import functools
import math

import jax
import jax.numpy as jnp
from jax import lax
from jax.experimental import pallas as pl
from jax.experimental.pallas import tpu as pltpu

F32 = jnp.float32
BF16 = jnp.bfloat16

HEAD_DIM = 64
MOBA_BLOCK = 256
MOBA_TOPK = 3
POOL_WINDOWS = (2, 4, 8, 16)
POOL_PREFIX = 16
PEER_HEADS = 8
PEER_NKEYS = 128
PEER_TOPK = 16
RMS_EPS = 1e-6
NEG = -1e30
LOG2E = 1.4426950408889634

LANES = 128
VMEM_LIMIT = 56 * 1024 * 1024

_NT = (((1,), (1,)), ((), ()))


def _params(semantics):
    return pltpu.CompilerParams(dimension_semantics=semantics, vmem_limit_bytes=VMEM_LIMIT)


def _rms(x, g):
    return x * lax.rsqrt(jnp.mean(x * x, axis=-1, keepdims=True) + RMS_EPS) * g


def _split_bf16(x):
    hi = x.astype(BF16)
    lo = (x - hi.astype(F32)).astype(BF16)
    return hi, lo


def _dot_nt_3pass(a, b):
    a_hi, a_lo = _split_bf16(a)
    b_hi, b_lo = _split_bf16(b)
    d = functools.partial(lax.dot_general, dimension_numbers=_NT, preferred_element_type=F32)
    return d(a_hi, b_hi) + (d(a_hi, b_lo) + d(a_lo, b_hi))


def _norm_proj_kernel(x_ref, g_ref, w_ref, q_ref, k_ref, v_ref, p_ref):
    h = _rms(x_ref[...], g_ref[...])
    proj = jnp.dot(h.astype(BF16), w_ref[...], preferred_element_type=F32)
    w = q_ref.shape[1]
    q_ref[...] = proj[:, 0 * w:1 * w]
    k_ref[...] = proj[:, 1 * w:2 * w]
    v_ref[...] = proj[:, 2 * w:3 * w]
    p_ref[...] = proj[:, 3 * w:4 * w]


def _norm_proj(x, g, w16, tm):
    t, d = x.shape
    w = w16.shape[1] // 4
    row = pl.BlockSpec((tm, d), lambda i: (i, 0))
    out = pl.BlockSpec((tm, w), lambda i: (i, 0))
    return pl.pallas_call(
        _norm_proj_kernel,
        grid=(t // tm,),
        in_specs=[row, pl.BlockSpec((1, d), lambda i: (0, 0)), pl.BlockSpec(w16.shape, lambda i: (0, 0))],
        out_specs=[out] * 4,
        out_shape=[jax.ShapeDtypeStruct((t, w), F32)] * 4,
        compiler_params=_params(("parallel",)),
        name="norm_proj",
    )(x, g.reshape(1, d), w16)


def _block_rank_select(gate, n_cand):
    lane = lax.broadcasted_iota(jnp.int32, gate.shape, 1)
    rank = jnp.zeros(gate.shape, F32)
    for m in range(n_cand):
        gm = gate[:, m:m + 1]
        beats = jnp.where(gm > gate, 1.0, jnp.where(gm == gate, jnp.where(lane > m, 1.0, 0.0), 0.0))
        rank = rank + beats
    return jnp.where(lane < n_cand, rank, float(MOBA_TOPK)) < float(MOBA_TOPK)


def _moba_prompt_kernel(q_ref, k_ref, v_ref, o_ref):
    s_len = q_ref.shape[1]
    nb = s_len // MOBA_BLOCK
    k = k_ref[0]
    k16 = k.astype(BF16)
    v16 = v_ref[0].astype(BF16)
    lane = lax.broadcasted_iota(jnp.int32, (1, LANES), 1)
    head_lanes = [lane < HEAD_DIM, lane >= HEAD_DIM]
    means = [jnp.mean(k[n * MOBA_BLOCK:(n + 1) * MOBA_BLOCK], axis=0, keepdims=True) for n in range(nb)]
    kbar = jnp.concatenate(means + [jnp.zeros((LANES - nb, LANES), F32)], axis=0)
    row = lax.broadcasted_iota(jnp.int32, (MOBA_BLOCK, MOBA_BLOCK), 0)
    col = lax.broadcasted_iota(jnp.int32, (MOBA_BLOCK, MOBA_BLOCK), 1)
    causal = row >= col
    scale = HEAD_DIM ** -0.5
    for i in range(nb):
        q = q_ref[0, i * MOBA_BLOCK:(i + 1) * MOBA_BLOCK, :]
        n_keys = (i + 1) * MOBA_BLOCK
        outs = []
        for hh in range(2):
            qh = jnp.where(head_lanes[hh], q, 0.0)
            s = lax.dot_general((qh * scale).astype(BF16), k16[:n_keys], _NT, preferred_element_type=F32)
            pieces = []
            if i > 0:
                gate = _dot_nt_3pass(qh, kbar)
                sel = jnp.where(_block_rank_select(gate, i), 1.0, 0.0)
                pieces = [jnp.where(sel[:, n:n + 1] > 0.5, s[:, n * MOBA_BLOCK:(n + 1) * MOBA_BLOCK], NEG)
                          for n in range(i)]
            pieces.append(jnp.where(causal, s[:, i * MOBA_BLOCK:], NEG))
            s = jnp.concatenate(pieces, axis=1)
            m = jnp.max(s, axis=-1, keepdims=True)
            p = jnp.exp(s - m)
            l = jnp.sum(p, axis=-1, keepdims=True)
            o = jnp.dot(p.astype(BF16), v16[:n_keys], preferred_element_type=F32)
            outs.append(o / l)
        o_ref[0, i * MOBA_BLOCK:(i + 1) * MOBA_BLOCK, :] = jnp.where(head_lanes[0], outs[0], outs[1])


def _moba_prompt(q, k, v):
    b, s, w = q.shape
    spec = pl.BlockSpec((1, s, LANES), lambda bi, pi: (bi, 0, pi))
    return pl.pallas_call(
        _moba_prompt_kernel,
        grid=(b, w // LANES),
        in_specs=[spec] * 3,
        out_specs=spec,
        out_shape=jax.ShapeDtypeStruct((b, s, w), F32),
        compiler_params=_params(("parallel", "parallel")),
        name="moba_prompt",
    )(q, k, v)


def _moba_sample_kernel(pt_ref, q_ref, kn_ref, vn_ref, *refs, blocks_per_step, pages_per_block):
    del pt_ref
    n_pg = blocks_per_step * pages_per_block
    kt_refs, vt_refs = refs[:n_pg], refs[n_pg:2 * n_pg]
    o_ref, m_scr, l_scr, g_scr, o_scr = refs[2 * n_pg:]
    step = pl.program_id(1)
    tq, w = q_ref.shape[1], q_ref.shape[2]
    n_heads = w // HEAD_DIM
    rows = tq * n_heads
    scale = HEAD_DIM ** -0.5

    head_of_lane = lax.broadcasted_iota(jnp.int32, (n_heads, w), 1) // HEAD_DIM
    head_mask = jnp.where(head_of_lane == lax.broadcasted_iota(jnp.int32, (n_heads, w), 0), 1.0, 0.0)
    q = q_ref[0]
    qbd = jnp.concatenate([jnp.broadcast_to(q[j:j + 1], (n_heads, w)) * head_mask for j in range(tq)], axis=0)

    @pl.when(step == 0)
    def _():
        m_scr[...] = jnp.zeros_like(m_scr)
        l_scr[...] = jnp.zeros_like(l_scr)
        g_scr[...] = jnp.zeros_like(g_scr)

    qs16 = (qbd * scale).astype(BF16)
    blk_lane = lax.broadcasted_iota(jnp.int32, (rows, LANES), 1)
    m_all, l_all, g_all = m_scr[...], l_scr[...], g_scr[...]
    for jj in range(blocks_per_step):
        n = step * blocks_per_step + jj
        pgs = range(jj * pages_per_block, (jj + 1) * pages_per_block)
        s = jnp.concatenate([jnp.dot(qs16, kt_refs[pg][0, 0].astype(BF16), preferred_element_type=F32)
                             for pg in pgs], axis=1)
        m_n = jnp.max(s, axis=-1, keepdims=True)
        p = jnp.exp(s - m_n)
        l_n = jnp.sum(p, axis=-1, keepdims=True)
        g_n = jnp.sum(s, axis=-1, keepdims=True)
        p16 = p.astype(BF16)
        page = kt_refs[0].shape[3]
        o_n = None
        for i, pg in enumerate(pgs):
            part = lax.dot_general(p16[:, i * page:(i + 1) * page], vt_refs[pg][0, 0].astype(BF16), _NT,
                                   preferred_element_type=F32)
            o_n = part if o_n is None else o_n + part
        o_scr[n] = o_n
        here = blk_lane == n
        m_all = jnp.where(here, m_n, m_all)
        l_all = jnp.where(here, l_n, l_all)
        g_all = jnp.where(here, g_n, g_all)
    m_scr[...] = m_all
    l_scr[...] = l_all
    g_scr[...] = g_all

    @pl.when(step == pl.num_programs(1) - 1)
    def _():
        n_past = o_scr.shape[0]
        sel = _block_rank_select(g_scr[...], n_past)
        m_all = m_scr[...]
        m_run = jnp.max(jnp.where(sel, m_all, NEG), axis=-1, keepdims=True)
        q_idx = lax.broadcasted_iota(jnp.int32, (rows, 1), 0) // n_heads
        kn = kn_ref[0]
        vn = vn_ref[0]
        s_own = []
        for j in range(tq):
            sj = jnp.sum(qbd * scale * kn[j:j + 1], axis=-1, keepdims=True)
            s_own.append(jnp.where(q_idx >= j, sj, NEG))
            m_run = jnp.maximum(m_run, s_own[j])
        wgt = jnp.where(sel, jnp.exp(m_all - m_run), 0.0)
        den = jnp.sum(wgt * l_scr[...], axis=-1, keepdims=True)
        num = jnp.zeros((rows, w), F32)
        for nn in range(n_past):
            num = num + wgt[:, nn:nn + 1] * o_scr[nn]
        for j in range(tq):
            e = jnp.where(q_idx >= j, jnp.exp(s_own[j] - m_run), 0.0)
            den = den + e
            num = num + e * vn[j:j + 1]
        acc = num / den
        out_rows = [jnp.sum(acc[j * n_heads:(j + 1) * n_heads] * head_mask, axis=0, keepdims=True)
                    for j in range(tq)]
        o_ref[0] = jnp.concatenate(out_rows, axis=0)


SAMPLE_BLOCKS_PER_STEP = 8


def _moba_sample(q, k_new, v_new, cache_kt, cache_vt, page_table, layer):
    db, tq, w = q.shape
    page = cache_kt.shape[3]
    n_pages = page_table.shape[1]
    ppb = MOBA_BLOCK // page
    assert MOBA_BLOCK == ppb * page and n_pages % (ppb * SAMPLE_BLOCKS_PER_STEP) == 0
    n_past = n_pages // ppb
    assert n_past <= LANES and tq <= MOBA_BLOCK
    rows = tq * (w // HEAD_DIM)
    n_pg = SAMPLE_BLOCKS_PER_STEP * ppb
    tok = pl.BlockSpec((1, tq, w), lambda b, s, pt: (b, 0, 0))

    def page_spec(which):
        return pl.BlockSpec((1, 1, w, page), lambda b, s, pt: (pt[b, s * n_pg + which], layer, 0, 0))

    pages = [page_spec(i) for i in range(n_pg)]
    grid_spec = pltpu.PrefetchScalarGridSpec(
        num_scalar_prefetch=1,
        grid=(db, n_past // SAMPLE_BLOCKS_PER_STEP),
        in_specs=[tok, tok, tok] + pages + pages,
        out_specs=tok,
        scratch_shapes=[pltpu.VMEM((rows, LANES), F32)] * 3 + [pltpu.VMEM((n_past, rows, w), F32)],
    )
    return pl.pallas_call(
        functools.partial(_moba_sample_kernel, blocks_per_step=SAMPLE_BLOCKS_PER_STEP, pages_per_block=ppb),
        grid_spec=grid_spec,
        out_shape=jax.ShapeDtypeStruct((db, tq, w), F32),
        compiler_params=_params(("parallel", "arbitrary")),
        name="moba_sample",
    )(page_table, q, k_new, v_new, *([cache_kt] * n_pg), *([cache_vt] * n_pg))


def _pool_mix_kernel(x_ref, pre_ref, w_ref, sc_ref, o_ref, xp_scr, *, pos0):
    t = x_ref.shape[1]
    gw = w_ref.shape[1]
    xp_scr[0:POOL_PREFIX, :] = pre_ref[0]
    xp_scr[POOL_PREFIX:POOL_PREFIX + t, :] = x_ref[0]
    pos = (lax.broadcasted_iota(jnp.int32, (t, 1), 0) + (pos0 + 1)).astype(F32)
    outs = []
    for g, win in enumerate(POOL_WINDOWS):
        sl = slice(g * gw, (g + 1) * gw)
        cur = xp_scr[POOL_PREFIX:POOL_PREFIX + t, sl]
        acc = cur
        for j in range(1, win):
            acc = acc + xp_scr[POOL_PREFIX - j:POOL_PREFIX - j + t, sl]
        r = acc / jnp.minimum(pos, float(win)) - cur
        outs.append(jnp.dot(r.astype(BF16), w_ref[g], preferred_element_type=F32))
    o_ref[0] = jnp.concatenate(outs, axis=1) * sc_ref[...]


def _pool_mix(pin, prefix, w_pool16, scale, pos0):
    b, t, w = pin.shape
    seq = pl.BlockSpec((1, t, w), lambda i: (i, 0, 0))
    return pl.pallas_call(
        functools.partial(_pool_mix_kernel, pos0=pos0),
        grid=(b,),
        in_specs=[seq, pl.BlockSpec((1, POOL_PREFIX, w), lambda i: (i, 0, 0)),
                  pl.BlockSpec(w_pool16.shape, lambda i: (0, 0, 0)), pl.BlockSpec((1, w), lambda i: (0, 0))],
        out_specs=seq,
        out_shape=jax.ShapeDtypeStruct((b, t, w), F32),
        scratch_shapes=[pltpu.VMEM((POOL_PREFIX + t, w), F32)],
        compiler_params=_params(("parallel",)),
        name="pool_mix",
    )(pin, prefix, w_pool16, scale.reshape(1, w))


def _mix_out_kernel(x_ref, a_ref, p_ref, wo_ref, g_ref, wq_ref, keys_ref, x1_ref, ht_ref, e1_ref, e2_ref, thr_ref):
    aw = a_ref.shape[1]
    mix = (jnp.dot(a_ref[...].astype(BF16), wo_ref[0:aw, :], preferred_element_type=F32)
           + jnp.dot(p_ref[...].astype(BF16), wo_ref[aw:, :], preferred_element_type=F32))
    x1 = x_ref[...] + mix
    x1_ref[...] = x1
    h = _rms(x1, g_ref[...])
    h16 = h.astype(BF16)
    ht_ref[...] = h.T.astype(BF16)
    qq = jnp.dot(h16, wq_ref[...], preferred_element_type=F32)
    half = keys_ref.shape[2]
    for hd in range(PEER_HEADS):
        q1 = qq[:, (2 * hd) * half:(2 * hd + 1) * half]
        q2 = qq[:, (2 * hd + 1) * half:(2 * hd + 2) * half]
        s1 = _dot_nt_3pass(keys_ref[0], q1)
        s2 = _dot_nt_3pass(keys_ref[1], q2)
        e1_ref[hd], e2_ref[hd], thr_ref[hd] = _gate_factors(s1, s2)


def _mix_out(x, attn, pool, wo16, g, wq16, keys, tm):
    t, d = x.shape
    aw = attn.shape[1]
    nk = keys.shape[1]
    row = lambda wdt: pl.BlockSpec((tm, wdt), lambda i: (i, 0))
    full = lambda a: pl.BlockSpec(a.shape, lambda i: (0,) * a.ndim)
    sc = pl.BlockSpec((PEER_HEADS, nk, tm), lambda i: (0, 0, i))
    return pl.pallas_call(
        _mix_out_kernel,
        grid=(t // tm,),
        in_specs=[row(d), row(aw), row(aw), full(wo16), pl.BlockSpec((1, d), lambda i: (0, 0)), full(wq16), full(keys)],
        out_specs=[row(d), pl.BlockSpec((d, tm), lambda i: (0, i)), sc, sc,
                   pl.BlockSpec((PEER_HEADS, 1, tm), lambda i: (0, 0, i))],
        out_shape=[jax.ShapeDtypeStruct((t, d), F32), jax.ShapeDtypeStruct((d, t), BF16),
                   jax.ShapeDtypeStruct((PEER_HEADS, nk, t), F32), jax.ShapeDtypeStruct((PEER_HEADS, nk, t), BF16),
                   jax.ShapeDtypeStruct((PEER_HEADS, 1, t), F32)],
        compiler_params=_params(("parallel",)),
        name="mix_out",
    )(x, attn, pool, wo16, g.reshape(1, d), wq16, keys)


def _top_values(s, count):
    vals = []
    cur = s
    for _ in range(count):
        mx = jnp.max(cur, axis=0, keepdims=True)
        vals.append(mx)
        cur = jnp.where(cur == mx, NEG, cur)
    return vals


def _compare_exchange(xs, i, j, descending=True):
    hi, lo = jnp.maximum(xs[i], xs[j]), jnp.minimum(xs[i], xs[j])
    xs[i], xs[j] = (hi, lo) if descending else (lo, hi)


def _bitonic_merge(xs):
    n = len(xs)
    j = n // 2
    while j >= 1:
        for i in range(n):
            if i ^ j > i:
                _compare_exchange(xs, i, i ^ j)
        j //= 2


def _top_sorted(s, count):
    sub = s.shape[0] // count
    xs = [s[k * sub:(k + 1) * sub] for k in range(count)]
    k = 2
    while k <= count:
        j = k // 2
        while j >= 1:
            for i in range(count):
                if i ^ j > i:
                    _compare_exchange(xs, i, i ^ j, descending=(i & k) == 0)
            j //= 2
        k *= 2
    shift = sub // 2
    while shift >= 1:
        ys = [pltpu.roll(x, shift, 0) for x in xs]
        xs = [jnp.maximum(xs[i], ys[count - 1 - i]) for i in range(count)]
        _bitonic_merge(xs)
        shift //= 2
    return [x[0:1] for x in xs]


def _gate_factors(s1, s2):
    v1 = _top_sorted(s1, PEER_TOPK)
    v2 = _top_sorted(s2, PEER_TOPK)
    pairs = [(a, b) for a in range(PEER_TOPK) for b in range(PEER_TOPK // (a + 1))]
    pad = [jnp.full_like(v1[0], NEG)] * (-len(pairs) % 8)
    cand = jnp.concatenate([v1[a] + v2[b] for a, b in pairs] + pad, axis=0)
    tau = _top_values(cand, PEER_TOPK)[-1]
    cmax = v1[0] + v2[0]
    chosen = cand >= tau
    z = jnp.sum(jnp.where(chosen, jnp.exp(cand - cmax), 0.0), axis=0, keepdims=True)
    e1 = lambda s: (jnp.exp(s - v1[0]) / z).astype(BF16)
    e2 = lambda s: jnp.exp(s - v2[0]).astype(BF16)
    e1_dense = jnp.where(s1 >= v1[-1], e1(s1).astype(F32), 0.0)
    e2_dense = jnp.where(s2 >= v2[-1], e2(s2), jnp.zeros_like(s2, BF16))
    v2m = jnp.concatenate(v2, axis=0)
    e2m = e2(v2m)
    thr = None
    for a in range(PEER_TOPK):
        prod = (e1(jnp.broadcast_to(v1[a], v2m.shape)) * e2m).astype(F32)
        low = jnp.min(jnp.where(v1[a] + v2m >= tau, prod, 1.0), axis=0, keepdims=True)
        thr = low if thr is None else jnp.minimum(thr, low)
    return e1_dense, e2_dense, thr


PEER_SUB_BLOCK = 256
PEER_PRE_BLOCK = 512


def _peer_dense_kernel(ht_ref, x_ref, u_ref, vt_ref, e1_ref, e2_ref, thr_ref, gf_ref, y_ref,
                       a_scr, w_scr, acc_scr, *, final_norm):
    j = pl.program_id(1)
    tt = ht_ref.shape[1]
    nk = e2_ref.shape[1]
    eb = u_ref.shape[0]

    @pl.when(j == 0)
    def _():
        acc_scr[...] = jnp.zeros_like(acc_scr)

    ht = ht_ref[...]
    zero = jnp.zeros((nk, LANES), BF16)
    n_sub = eb // PEER_SUB_BLOCK

    subs_per_pre = PEER_PRE_BLOCK // PEER_SUB_BLOCK

    def pre_activations(pb):
        blk = slice(pb * PEER_PRE_BLOCK, (pb + 1) * PEER_PRE_BLOCK)
        a_scr[blk, :] = jnp.dot(u_ref[blk, :], ht, preferred_element_type=F32)

    def gated_weights(sb):
        for c in range(sb * PEER_SUB_BLOCK // nk, (sb + 1) * PEER_SUB_BLOCK // nk):
            rows = slice(c * nk, (c + 1) * nk)
            for lt in range(tt // LANES):
                ls = slice(lt * LANES, (lt + 1) * LANES)
                gate = None
                for hd in range(PEER_HEADS):
                    e1 = jnp.broadcast_to(e1_ref[hd, c:c + 1, ls], (nk, LANES)).astype(BF16)
                    thr = jnp.broadcast_to(thr_ref[hd, :, ls], (nk, LANES)).astype(BF16)
                    p = e1 * e2_ref[hd, :, ls]
                    g = jnp.where(p >= thr, p, zero)
                    gate = g if gate is None else gate + g
                a = a_scr[rows, ls]
                gelu = 0.5 * a * (1.0 + lax.erf(a * (2.0 ** -0.5)))
                w_scr[rows, ls] = gate * gelu.astype(BF16)

    pre_activations(0)
    for sb in range(n_sub):
        if (sb + 1) % subs_per_pre == 0 and sb + 1 < n_sub:
            pre_activations((sb + 1) // subs_per_pre)
        gated_weights(sb)
        sub = slice(sb * PEER_SUB_BLOCK, (sb + 1) * PEER_SUB_BLOCK)
        acc_scr[...] += jnp.dot(vt_ref[0, :, sub], w_scr[sub, :], preferred_element_type=F32)

    @pl.when(j == pl.num_programs(1) - 1)
    def _():
        y = x_ref[...] + acc_scr[...].T
        y_ref[...] = _rms(y, gf_ref[...]) if final_norm else y


PEER_EXPERT_BLOCK = 2048


def _peer_dense(ht16, x, u16, vt16, e1t, e2t, thr, g_final, final_norm, tt):
    t, d = x.shape
    e = u16.shape[0]
    eb = PEER_EXPERT_BLOCK
    nh, nk, _ = e1t.shape
    assert eb % PEER_SUB_BLOCK == 0 and PEER_SUB_BLOCK % nk == 0 and vt16.shape == (e // eb, d, eb)
    tok = lambda: pl.BlockSpec((tt, d), lambda i, j: (i, 0))
    return pl.pallas_call(
        functools.partial(_peer_dense_kernel, final_norm=final_norm),
        grid=(t // tt, e // eb),
        in_specs=[pl.BlockSpec((d, tt), lambda i, j: (0, i)), tok(),
                  pl.BlockSpec((eb, d), lambda i, j: (j, 0)),
                  pl.BlockSpec((1, d, eb), lambda i, j: (j, 0, 0)),
                  pl.BlockSpec((nh, eb // nk, tt), lambda i, j: (0, j, i)),
                  pl.BlockSpec((nh, nk, tt), lambda i, j: (0, 0, i)),
                  pl.BlockSpec((nh, 1, tt), lambda i, j: (0, 0, i)),
                  pl.BlockSpec((1, d), lambda i, j: (0, 0))],
        out_specs=tok(),
        out_shape=jax.ShapeDtypeStruct((t, d), F32),
        scratch_shapes=[pltpu.VMEM((eb, tt), F32), pltpu.VMEM((eb, tt), BF16), pltpu.VMEM((d, tt), F32)],
        compiler_params=_params(("parallel", "arbitrary")),
        name="peer_dense",
    )(ht16, x, u16, vt16, e1t, e2t, thr, g_final.reshape(1, d))


def _token_tile(t, want):
    return want if t % want == 0 else t


def _ffn(x, attn, pool, wo16, g_ffn, wq16, keys, u16, vt16, g_final, final_norm):
    t = x.shape[0]
    x1, ht16, e1t, e2t, thr = _mix_out(x, attn, pool, wo16, g_ffn, wq16, keys, _token_tile(t, 256))
    return _peer_dense(ht16, x1, u16, vt16, e1t, e2t, thr, g_final, final_norm, _token_tile(t, 512))


def kernel(x_prompt, x_sample, cache_k, cache_v, state_pool, page_table, norm_mix, w_in, w_pool, pool_scale,
           w_out, norm_ffn, peer_wq, peer_keys, peer_u, peer_v, norm_final):
    bp, sp, d = x_prompt.shape
    db, ts, _ = x_sample.shape
    depth = w_in.shape[0]
    aw = w_in.shape[2] // 4
    n_heads = aw // HEAD_DIM
    pool_buf = state_pool.shape[2]
    assert pool_buf == POOL_PREFIX - 1 and sp % MOBA_BLOCK == 0 and aw % LANES == 0
    n_pool, _, page, _, _ = cache_k.shape
    ck = cache_k.transpose(0, 1, 3, 4, 2).reshape(n_pool, depth, aw, page)
    cv = cache_v.transpose(0, 1, 3, 4, 2).reshape(n_pool, depth, aw, page)
    past_len = page_table.shape[1] * page
    ts_pad = -(-ts // 8) * 8

    y_p = x_prompt.reshape(bp * sp, d)
    y_s = x_sample.reshape(db * ts, d)
    kp, vp, pp, kss, vss, pss = [], [], [], [], [], []
    for l in range(depth):
        w_in16 = w_in[l].astype(BF16)
        w_pool16 = w_pool[l].astype(BF16)
        wo16 = w_out[l].astype(BF16)
        wq16 = peer_wq[l].astype(BF16)
        u16 = peer_u[l].astype(BF16)
        n_eb = peer_v.shape[1] // PEER_EXPERT_BLOCK
        vt16 = peer_v[l].astype(BF16).reshape(n_eb, PEER_EXPERT_BLOCK, d).transpose(0, 2, 1)
        q, k, v, pin = _norm_proj(y_p, norm_mix[l], w_in16, 512)
        q3, k3, v3, pin3 = (a.reshape(bp, sp, aw) for a in (q, k, v, pin))
        attn = _moba_prompt(q3, k3, v3)
        pool = _pool_mix(pin3, jnp.zeros((bp, POOL_PREFIX, aw), F32), w_pool16, pool_scale[l], 0)
        y_p = _ffn(y_p, attn.reshape(bp * sp, aw), pool.reshape(bp * sp, aw), wo16, norm_ffn[l], wq16,
                   peer_keys[l], u16, vt16, norm_final, l == depth - 1)
        kp.append(k3.reshape(bp, sp, n_heads, HEAD_DIM))
        vp.append(v3.reshape(bp, sp, n_heads, HEAD_DIM))
        pp.append(pin3[:, sp - pool_buf:])
        q, k, v, pin = _norm_proj(y_s, norm_mix[l], w_in16, db * ts)
        q3, k3, v3, pin3 = (a.reshape(db, ts, aw) for a in (q, k, v, pin))
        attn = _moba_sample(q3, k3, v3, ck, cv, page_table, l)
        prefix = jnp.concatenate([jnp.zeros((db, 1, aw), F32), state_pool[l]], axis=1)
        pin_pad = jnp.pad(pin3, ((0, 0), (0, ts_pad - ts), (0, 0)))
        pool = _pool_mix(pin_pad, prefix, w_pool16, pool_scale[l], past_len)[:, :ts]
        y_s = _ffn(y_s, attn.reshape(db * ts, aw), pool.reshape(db * ts, aw), wo16, norm_ffn[l], wq16,
                   peer_keys[l], u16, vt16, norm_final, l == depth - 1)
        kss.append(k3.reshape(db, ts, n_heads, HEAD_DIM))
        vss.append(v3.reshape(db, ts, n_heads, HEAD_DIM))
        pss.append(jnp.concatenate([state_pool[l], pin3], axis=1)[:, ts:])
    y_prompt = y_p.reshape(bp, sp, d)
    y_sample = y_s.reshape(db, ts, d)
    return (y_prompt, y_sample, jnp.stack(kp, axis=1), jnp.stack(vp, axis=1), jnp.stack(pp, axis=0),
            jnp.stack(kss, axis=1), jnp.stack(vss, axis=1), jnp.stack(pss, axis=0))
```

```python
import functools

import jax
import jax.numpy as jnp
from jax import lax
from jax.experimental import pallas as pl
from jax.experimental.pallas import tpu as pltpu

F32 = jnp.float32
BF16 = jnp.bfloat16

HEAD_DIM = 64
MOBA_BLOCK = 256
MOBA_TOPK = 3
POOL_WINDOWS = (2, 4, 8, 16)
POOL_PREFIX = 16
PEER_HEADS = 8
PEER_TOPK = 16
RMS_EPS = 1e-6
NEG = -1e30

LANES = 128
VMEM_LIMIT = 56 * 1024 * 1024

_NT = (((1,), (1,)), ((), ()))


def _params(semantics):
    return pltpu.CompilerParams(dimension_semantics=semantics, vmem_limit_bytes=VMEM_LIMIT)


def _rms(x, g):
    return x * lax.rsqrt(jnp.mean(x * x, axis=-1, keepdims=True) + RMS_EPS) * g


def _split_bf16(x):
    hi = x.astype(BF16)
    lo = (x - hi.astype(F32)).astype(BF16)
    return hi, lo


def _dot_nt_3pass(a, b):
    a_hi, a_lo = _split_bf16(a)
    b_hi, b_lo = _split_bf16(b)
    d = functools.partial(lax.dot_general, dimension_numbers=_NT, preferred_element_type=F32)
    return d(a_hi, b_hi) + (d(a_hi, b_lo) + d(a_lo, b_hi))


def _norm_proj_kernel(x_ref, g_ref, w_ref, q_ref, k_ref, v_ref, p_ref):
    h = _rms(x_ref[...], g_ref[...])
    proj = jnp.dot(h.astype(BF16), w_ref[...], preferred_element_type=F32)
    w = q_ref.shape[1]
    q_ref[...] = proj[:, 0 * w:1 * w]
    k_ref[...] = proj[:, 1 * w:2 * w]
    v_ref[...] = proj[:, 2 * w:3 * w]
    p_ref[...] = proj[:, 3 * w:4 * w]


def _norm_proj(x, g, w16, tm):
    t, d = x.shape
    w = w16.shape[1] // 4
    row = pl.BlockSpec((tm, d), lambda i: (i, 0))
    out = pl.BlockSpec((tm, w), lambda i: (i, 0))
    return pl.pallas_call(
        _norm_proj_kernel,
        grid=(t // tm,),
        in_specs=[row, pl.BlockSpec((1, d), lambda i: (0, 0)), pl.BlockSpec(w16.shape, lambda i: (0, 0))],
        out_specs=[out] * 4,
        out_shape=[jax.ShapeDtypeStruct((t, w), F32)] * 4,
        compiler_params=_params(("parallel",)),
        name="norm_proj",
    )(x, g.reshape(1, d), w16)


def _block_rank_select(gate, n_cand):
    lane = lax.broadcasted_iota(jnp.int32, gate.shape, 1)
    rank = jnp.zeros(gate.shape, F32)
    for m in range(n_cand):
        gm = gate[:, m:m + 1]
        beats = jnp.where(gm > gate, 1.0, jnp.where(gm == gate, jnp.where(lane > m, 1.0, 0.0), 0.0))
        rank = rank + beats
    return jnp.where(lane < n_cand, rank, float(MOBA_TOPK)) < float(MOBA_TOPK)


def _moba_prompt_kernel(q_ref, k_ref, v_ref, o_ref):
    s_len = q_ref.shape[1]
    nb = s_len // MOBA_BLOCK
    k = k_ref[0]
    k16 = k.astype(BF16)
    v16 = v_ref[0].astype(BF16)
    lane = lax.broadcasted_iota(jnp.int32, (1, LANES), 1)
    head_lanes = [lane < HEAD_DIM, lane >= HEAD_DIM]
    means = [jnp.mean(k[n * MOBA_BLOCK:(n + 1) * MOBA_BLOCK], axis=0, keepdims=True) for n in range(nb)]
    kbar = jnp.concatenate(means + [jnp.zeros((LANES - nb, LANES), F32)], axis=0)
    row = lax.broadcasted_iota(jnp.int32, (MOBA_BLOCK, MOBA_BLOCK), 0)
    col = lax.broadcasted_iota(jnp.int32, (MOBA_BLOCK, MOBA_BLOCK), 1)
    causal = row >= col
    scale = HEAD_DIM ** -0.5
    for i in range(nb):
        q = q_ref[0, i * MOBA_BLOCK:(i + 1) * MOBA_BLOCK, :]
        n_keys = (i + 1) * MOBA_BLOCK
        outs = []
        for hh in range(2):
            qh = jnp.where(head_lanes[hh], q, 0.0)
            s = lax.dot_general((qh * scale).astype(BF16), k16[:n_keys], _NT, preferred_element_type=F32)
            pieces = []
            if i > 0:
                gate = _dot_nt_3pass(qh, kbar)
                sel = jnp.where(_block_rank_select(gate, i), 1.0, 0.0)
                pieces = [jnp.where(sel[:, n:n + 1] > 0.5, s[:, n * MOBA_BLOCK:(n + 1) * MOBA_BLOCK], NEG)
                          for n in range(i)]
            pieces.append(jnp.where(causal, s[:, i * MOBA_BLOCK:], NEG))
            s = jnp.concatenate(pieces, axis=1)
            m = jnp.max(s, axis=-1, keepdims=True)
            p = jnp.exp(s - m)
            l = jnp.sum(p, axis=-1, keepdims=True)
            o = jnp.dot(p.astype(BF16), v16[:n_keys], preferred_element_type=F32)
            outs.append(o / l)
        o_ref[0, i * MOBA_BLOCK:(i + 1) * MOBA_BLOCK, :] = jnp.where(head_lanes[0], outs[0], outs[1])


def _moba_prompt(q, k, v):
    b, s, w = q.shape
    spec = pl.BlockSpec((1, s, LANES), lambda bi, pi: (bi, 0, pi))
    return pl.pallas_call(
        _moba_prompt_kernel,
        grid=(b, w // LANES),
        in_specs=[spec] * 3,
        out_specs=spec,
        out_shape=jax.ShapeDtypeStruct((b, s, w), F32),
        compiler_params=_params(("parallel", "parallel")),
        name="moba_prompt",
    )(q, k, v)


def _moba_sample_kernel(pt_ref, q_ref, kn_ref, vn_ref, *refs, blocks_per_step, pages_per_block):
    del pt_ref
    n_pg = blocks_per_step * pages_per_block
    kt_refs, vt_refs = refs[:n_pg], refs[n_pg:2 * n_pg]
    o_ref, m_scr, l_scr, g_scr, o_scr = refs[2 * n_pg:]
    step = pl.program_id(1)
    tq, w = q_ref.shape[1], q_ref.shape[2]
    n_heads = w // HEAD_DIM
    rows = tq * n_heads
    scale = HEAD_DIM ** -0.5

    head_of_lane = lax.broadcasted_iota(jnp.int32, (n_heads, w), 1) // HEAD_DIM
    head_mask = jnp.where(head_of_lane == lax.broadcasted_iota(jnp.int32, (n_heads, w), 0), 1.0, 0.0)
    q = q_ref[0]
    qbd = jnp.concatenate([jnp.broadcast_to(q[j:j + 1], (n_heads, w)) * head_mask for j in range(tq)], axis=0)

    @pl.when(step == 0)
    def _():
        m_scr[...] = jnp.zeros_like(m_scr)
        l_scr[...] = jnp.zeros_like(l_scr)
        g_scr[...] = jnp.zeros_like(g_scr)

    qs16 = (qbd * scale).astype(BF16)
    blk_lane = lax.broadcasted_iota(jnp.int32, (rows, LANES), 1)
    m_all, l_all, g_all = m_scr[...], l_scr[...], g_scr[...]
    for jj in range(blocks_per_step):
        n = step * blocks_per_step + jj
        pgs = range(jj * pages_per_block, (jj + 1) * pages_per_block)
        s = jnp.concatenate([jnp.dot(qs16, kt_refs[pg][0, 0].astype(BF16), preferred_element_type=F32)
                             for pg in pgs], axis=1)
        m_n = jnp.max(s, axis=-1, keepdims=True)
        p = jnp.exp(s - m_n)
        l_n = jnp.sum(p, axis=-1, keepdims=True)
        g_n = jnp.sum(s, axis=-1, keepdims=True)
        p16 = p.astype(BF16)
        page = kt_refs[0].shape[3]
        o_n = None
        for i, pg in enumerate(pgs):
            part = lax.dot_general(p16[:, i * page:(i + 1) * page], vt_refs[pg][0, 0].astype(BF16), _NT,
                                   preferred_element_type=F32)
            o_n = part if o_n is None else o_n + part
        o_scr[n] = o_n
        here = blk_lane == n
        m_all = jnp.where(here, m_n, m_all)
        l_all = jnp.where(here, l_n, l_all)
        g_all = jnp.where(here, g_n, g_all)
    m_scr[...] = m_all
    l_scr[...] = l_all
    g_scr[...] = g_all

    @pl.when(step == pl.num_programs(1) - 1)
    def _():
        n_past = o_scr.shape[0]
        sel = _block_rank_select(g_scr[...], n_past)
        m_all = m_scr[...]
        m_run = jnp.max(jnp.where(sel, m_all, NEG), axis=-1, keepdims=True)
        q_idx = lax.broadcasted_iota(jnp.int32, (rows, 1), 0) // n_heads
        kn = kn_ref[0]
        vn = vn_ref[0]
        s_own = []
        for j in range(tq):
            sj = jnp.sum(qbd * scale * kn[j:j + 1], axis=-1, keepdims=True)
            s_own.append(jnp.where(q_idx >= j, sj, NEG))
            m_run = jnp.maximum(m_run, s_own[j])
        wgt = jnp.where(sel, jnp.exp(m_all - m_run), 0.0)
        den = jnp.sum(wgt * l_scr[...], axis=-1, keepdims=True)
        num = jnp.zeros((rows, w), F32)
        for nn in range(n_past):
            num = num + wgt[:, nn:nn + 1] * o_scr[nn]
        for j in range(tq):
            e = jnp.where(q_idx >= j, jnp.exp(s_own[j] - m_run), 0.0)
            den = den + e
            num = num + e * vn[j:j + 1]
        acc = num / den
        out_rows = [jnp.sum(acc[j * n_heads:(j + 1) * n_heads] * head_mask, axis=0, keepdims=True)
                    for j in range(tq)]
        o_ref[0] = jnp.concatenate(out_rows, axis=0)


SAMPLE_BLOCKS_PER_STEP = 8


def _moba_sample(q, k_new, v_new, cache_kt, cache_vt, page_table, layer):
    db, tq, w = q.shape
    page = cache_kt.shape[3]
    n_pages = page_table.shape[1]
    ppb = MOBA_BLOCK // page
    assert MOBA_BLOCK == ppb * page and n_pages % (ppb * SAMPLE_BLOCKS_PER_STEP) == 0
    n_past = n_pages // ppb
    assert n_past <= LANES and tq <= MOBA_BLOCK
    rows = tq * (w // HEAD_DIM)
    n_pg = SAMPLE_BLOCKS_PER_STEP * ppb
    tok = pl.BlockSpec((1, tq, w), lambda b, s, pt: (b, 0, 0))

    def page_spec(which):
        return pl.BlockSpec((1, 1, w, page), lambda b, s, pt: (pt[b, s * n_pg + which], layer, 0, 0))

    pages = [page_spec(i) for i in range(n_pg)]
    grid_spec = pltpu.PrefetchScalarGridSpec(
        num_scalar_prefetch=1,
        grid=(db, n_past // SAMPLE_BLOCKS_PER_STEP),
        in_specs=[tok, tok, tok] + pages + pages,
        out_specs=tok,
        scratch_shapes=[pltpu.VMEM((rows, LANES), F32)] * 3 + [pltpu.VMEM((n_past, rows, w), F32)],
    )
    return pl.pallas_call(
        functools.partial(_moba_sample_kernel, blocks_per_step=SAMPLE_BLOCKS_PER_STEP, pages_per_block=ppb),
        grid_spec=grid_spec,
        out_shape=jax.ShapeDtypeStruct((db, tq, w), F32),
        compiler_params=_params(("parallel", "arbitrary")),
        name="moba_sample",
    )(page_table, q, k_new, v_new, *([cache_kt] * n_pg), *([cache_vt] * n_pg))


def _pool_mix_kernel(x_ref, pre_ref, w_ref, sc_ref, o_ref, xp_scr, *, pos0):
    t = x_ref.shape[1]
    gw = w_ref.shape[1]
    xp_scr[0:POOL_PREFIX, :] = pre_ref[0]
    xp_scr[POOL_PREFIX:POOL_PREFIX + t, :] = x_ref[0]
    pos = (lax.broadcasted_iota(jnp.int32, (t, 1), 0) + (pos0 + 1)).astype(F32)
    outs = []
    for g, win in enumerate(POOL_WINDOWS):
        sl = slice(g * gw, (g + 1) * gw)
        cur = xp_scr[POOL_PREFIX:POOL_PREFIX + t, sl]
        acc = cur
        for j in range(1, win):
            acc = acc + xp_scr[POOL_PREFIX - j:POOL_PREFIX - j + t, sl]
        r = acc / jnp.minimum(pos, float(win)) - cur
        outs.append(jnp.dot(r.astype(BF16), w_ref[g], preferred_element_type=F32))
    o_ref[0] = jnp.concatenate(outs, axis=1) * sc_ref[...]


def _pool_mix(pin, prefix, w_pool16, scale, pos0):
    b, t, w = pin.shape
    seq = pl.BlockSpec((1, t, w), lambda i: (i, 0, 0))
    return pl.pallas_call(
        functools.partial(_pool_mix_kernel, pos0=pos0),
        grid=(b,),
        in_specs=[seq, pl.BlockSpec((1, POOL_PREFIX, w), lambda i: (i, 0, 0)),
                  pl.BlockSpec(w_pool16.shape, lambda i: (0, 0, 0)), pl.BlockSpec((1, w), lambda i: (0, 0))],
        out_specs=seq,
        out_shape=jax.ShapeDtypeStruct((b, t, w), F32),
        scratch_shapes=[pltpu.VMEM((POOL_PREFIX + t, w), F32)],
        compiler_params=_params(("parallel",)),
        name="pool_mix",
    )(pin, prefix, w_pool16, scale.reshape(1, w))


def _mix_out_kernel(x_ref, a_ref, p_ref, wo_ref, g_ref, wq_ref, keys_ref, x1_ref, ht_ref, e1_ref, e2_ref, thr_ref):
    aw = a_ref.shape[1]
    mix = (jnp.dot(a_ref[...].astype(BF16), wo_ref[0:aw, :], preferred_element_type=F32)
           + jnp.dot(p_ref[...].astype(BF16), wo_ref[aw:, :], preferred_element_type=F32))
    x1 = x_ref[...] + mix
    x1_ref[...] = x1
    h = _rms(x1, g_ref[...])
    h16 = h.astype(BF16)
    ht_ref[...] = h.T.astype(BF16)
    qq = jnp.dot(h16, wq_ref[...], preferred_element_type=F32)
    half = keys_ref.shape[2]
    for hd in range(PEER_HEADS):
        q1 = qq[:, (2 * hd) * half:(2 * hd + 1) * half]
        q2 = qq[:, (2 * hd + 1) * half:(2 * hd + 2) * half]
        s1 = _dot_nt_3pass(keys_ref[0], q1)
        s2 = _dot_nt_3pass(keys_ref[1], q2)
        e1_ref[hd], e2_ref[hd], thr_ref[hd] = _gate_factors(s1, s2)


def _mix_out(x, attn, pool, wo16, g, wq16, keys, tm):
    t, d = x.shape
    aw = attn.shape[1]
    nk = keys.shape[1]
    row = lambda wdt: pl.BlockSpec((tm, wdt), lambda i: (i, 0))
    full = lambda a: pl.BlockSpec(a.shape, lambda i: (0,) * a.ndim)
    sc = pl.BlockSpec((PEER_HEADS, nk, tm), lambda i: (0, 0, i))
    return pl.pallas_call(
        _mix_out_kernel,
        grid=(t // tm,),
        in_specs=[row(d), row(aw), row(aw), full(wo16), pl.BlockSpec((1, d), lambda i: (0, 0)), full(wq16), full(keys)],
        out_specs=[row(d), pl.BlockSpec((d, tm), lambda i: (0, i)), sc, sc,
                   pl.BlockSpec((PEER_HEADS, 1, tm), lambda i: (0, 0, i))],
        out_shape=[jax.ShapeDtypeStruct((t, d), F32), jax.ShapeDtypeStruct((d, t), BF16),
                   jax.ShapeDtypeStruct((PEER_HEADS, nk, t), F32), jax.ShapeDtypeStruct((PEER_HEADS, nk, t), BF16),
                   jax.ShapeDtypeStruct((PEER_HEADS, 1, t), F32)],
        compiler_params=_params(("parallel",)),
        name="mix_out",
    )(x, attn, pool, wo16, g.reshape(1, d), wq16, keys)


def _top_values(s, count):
    vals = []
    cur = s
    for _ in range(count):
        mx = jnp.max(cur, axis=0, keepdims=True)
        vals.append(mx)
        cur = jnp.where(cur == mx, NEG, cur)
    return vals


def _compare_exchange(xs, i, j, descending=True):
    hi, lo = jnp.maximum(xs[i], xs[j]), jnp.minimum(xs[i], xs[j])
    xs[i], xs[j] = (hi, lo) if descending else (lo, hi)


def _bitonic_merge(xs):
    n = len(xs)
    j = n // 2
    while j >= 1:
        for i in range(n):
            if i ^ j > i:
                _compare_exchange(xs, i, i ^ j)
        j //= 2


def _top_sorted(s, count):
    sub = s.shape[0] // count
    xs = [s[k * sub:(k + 1) * sub] for k in range(count)]
    k = 2
    while k <= count:
        j = k // 2
        while j >= 1:
            for i in range(count):
                if i ^ j > i:
                    _compare_exchange(xs, i, i ^ j, descending=(i & k) == 0)
            j //= 2
        k *= 2
    shift = sub // 2
    while shift >= 1:
        ys = [pltpu.roll(x, shift, 0) for x in xs]
        xs = [jnp.maximum(xs[i], ys[count - 1 - i]) for i in range(count)]
        _bitonic_merge(xs)
        shift //= 2
    return [x[0:1] for x in xs]


def _gate_factors(s1, s2):
    v1 = _top_sorted(s1, PEER_TOPK)
    v2 = _top_sorted(s2, PEER_TOPK)
    pairs = [(a, b) for a in range(PEER_TOPK) for b in range(PEER_TOPK // (a + 1))]
    pad = [jnp.full_like(v1[0], NEG)] * (-len(pairs) % 8)
    cand = jnp.concatenate([v1[a] + v2[b] for a, b in pairs] + pad, axis=0)
    tau = _top_values(cand, PEER_TOPK)[-1]
    cmax = v1[0] + v2[0]
    chosen = cand >= tau
    z = jnp.sum(jnp.where(chosen, jnp.exp(cand - cmax), 0.0), axis=0, keepdims=True)
    e1 = lambda s: (jnp.exp(s - v1[0]) / z).astype(BF16)
    e2 = lambda s: jnp.exp(s - v2[0]).astype(BF16)
    e1_dense = jnp.where(s1 >= v1[-1], e1(s1).astype(F32), 0.0)
    e2_dense = jnp.where(s2 >= v2[-1], e2(s2), jnp.zeros_like(s2, BF16))
    v2m = jnp.concatenate(v2, axis=0)
    e2m = e2(v2m)
    thr = None
    for a in range(PEER_TOPK):
        prod = (e1(jnp.broadcast_to(v1[a], v2m.shape)) * e2m).astype(F32)
        low = jnp.min(jnp.where(v1[a] + v2m >= tau, prod, 1.0), axis=0, keepdims=True)
        thr = low if thr is None else jnp.minimum(thr, low)
    return e1_dense, e2_dense, thr


PEER_SUB_BLOCK = 512
PEER_PRE_BLOCK = 512


def _peer_dense_kernel(ht_ref, x_ref, u_ref, vt_ref, e1_ref, e2_ref, thr_ref, gf_ref, y_ref,
                       a_scr, w_scr, acc_scr, *, final_norm):
    j = pl.program_id(1)
    tt = ht_ref.shape[1]
    nk = e2_ref.shape[1]
    eb = u_ref.shape[0]

    @pl.when(j == 0)
    def _():
        acc_scr[...] = jnp.zeros_like(acc_scr)

    ht = ht_ref[...]
    zero = jnp.zeros((nk, LANES), BF16)
    n_sub = eb // PEER_SUB_BLOCK

    subs_per_pre = PEER_PRE_BLOCK // PEER_SUB_BLOCK

    def pre_activations(pb):
        blk = slice(pb * PEER_PRE_BLOCK, (pb + 1) * PEER_PRE_BLOCK)
        a_scr[blk, :] = jnp.dot(u_ref[blk, :], ht, preferred_element_type=F32)

    def gated_weights(sb):
        for c in range(sb * PEER_SUB_BLOCK // nk, (sb + 1) * PEER_SUB_BLOCK // nk):
            rows = slice(c * nk, (c + 1) * nk)
            for lt in range(tt // LANES):
                ls = slice(lt * LANES, (lt + 1) * LANES)
                gate = None
                for hd in range(PEER_HEADS):
                    e1 = jnp.broadcast_to(e1_ref[hd, c:c + 1, ls], (nk, LANES)).astype(BF16)
                    thr = jnp.broadcast_to(thr_ref[hd, :, ls], (nk, LANES)).astype(BF16)
                    p = e1 * e2_ref[hd, :, ls]
                    g = jnp.where(p >= thr, p, zero)
                    gate = g if gate is None else gate + g
                a = a_scr[rows, ls]
                gelu = 0.5 * a * (1.0 + lax.erf(a * (2.0 ** -0.5)))
                w_scr[rows, ls] = gate * gelu.astype(BF16)

    pre_activations(0)
    for sb in range(n_sub):
        if (sb + 1) % subs_per_pre == 0 and sb + 1 < n_sub:
            pre_activations((sb + 1) // subs_per_pre)
        gated_weights(sb)
        sub = slice(sb * PEER_SUB_BLOCK, (sb + 1) * PEER_SUB_BLOCK)
        acc_scr[...] += jnp.dot(vt_ref[0, :, sub], w_scr[sub, :], preferred_element_type=F32)

    @pl.when(j == pl.num_programs(1) - 1)
    def _():
        y = x_ref[...] + acc_scr[...].T
        y_ref[...] = _rms(y, gf_ref[...]) if final_norm else y


PEER_EXPERT_BLOCK = 2048


def _peer_dense(ht16, x, u16, vt16, e1t, e2t, thr, g_final, final_norm, tt):
    t, d = x.shape
    e = u16.shape[0]
    eb = PEER_EXPERT_BLOCK
    nh, nk, _ = e1t.shape
    assert eb % PEER_SUB_BLOCK == 0 and PEER_SUB_BLOCK % nk == 0 and vt16.shape == (e // eb, d, eb)
    tok = lambda: pl.BlockSpec((tt, d), lambda i, j: (i, 0))
    return pl.pallas_call(
        functools.partial(_peer_dense_kernel, final_norm=final_norm),
        grid=(t // tt, e // eb),
        in_specs=[pl.BlockSpec((d, tt), lambda i, j: (0, i)), tok(),
                  pl.BlockSpec((eb, d), lambda i, j: (j, 0)),
                  pl.BlockSpec((1, d, eb), lambda i, j: (j, 0, 0)),
                  pl.BlockSpec((nh, eb // nk, tt), lambda i, j: (0, j, i)),
                  pl.BlockSpec((nh, nk, tt), lambda i, j: (0, 0, i)),
                  pl.BlockSpec((nh, 1, tt), lambda i, j: (0, 0, i)),
                  pl.BlockSpec((1, d), lambda i, j: (0, 0))],
        out_specs=tok(),
        out_shape=jax.ShapeDtypeStruct((t, d), F32),
        scratch_shapes=[pltpu.VMEM((eb, tt), F32), pltpu.VMEM((eb, tt), BF16), pltpu.VMEM((d, tt), F32)],
        compiler_params=_params(("parallel", "arbitrary")),
        name="peer_dense",
    )(ht16, x, u16, vt16, e1t, e2t, thr, g_final.reshape(1, d))


def _token_tile(t, want):
    return want if t % want == 0 else t


def _ffn(x, attn, pool, wo16, g_ffn, wq16, keys, u16, vt16, g_final, final_norm):
    t = x.shape[0]
    x1, ht16, e1t, e2t, thr = _mix_out(x, attn, pool, wo16, g_ffn, wq16, keys, _token_tile(t, 256))
    return _peer_dense(ht16, x1, u16, vt16, e1t, e2t, thr, g_final, final_norm, _token_tile(t, 512))


def kernel(x_prompt, x_sample, cache_k, cache_v, state_pool, page_table, norm_mix, w_in, w_pool, pool_scale,
           w_out, norm_ffn, peer_wq, peer_keys, peer_u, peer_v, norm_final):
    bp, sp, d = x_prompt.shape
    db, ts, _ = x_sample.shape
    depth = w_in.shape[0]
    aw = w_in.shape[2] // 4
    n_heads = aw // HEAD_DIM
    pool_buf = state_pool.shape[2]
    assert pool_buf == POOL_PREFIX - 1 and sp % MOBA_BLOCK == 0 and aw % LANES == 0
    n_pool, _, page, _, _ = cache_k.shape
    ck = cache_k.transpose(0, 1, 3, 4, 2).reshape(n_pool, depth, aw, page)
    cv = cache_v.transpose(0, 1, 3, 4, 2).reshape(n_pool, depth, aw, page)
    past_len = page_table.shape[1] * page
    ts_pad = -(-ts // 8) * 8

    y_p = x_prompt.reshape(bp * sp, d)
    y_s = x_sample.reshape(db * ts, d)
    kp, vp, pp, kss, vss, pss = [], [], [], [], [], []
    for l in range(depth):
        w_in16 = w_in[l].astype(BF16)
        w_pool16 = w_pool[l].astype(BF16)
        wo16 = w_out[l].astype(BF16)
        wq16 = peer_wq[l].astype(BF16)
        u16 = peer_u[l].astype(BF16)
        n_eb = peer_v.shape[1] // PEER_EXPERT_BLOCK
        vt16 = peer_v[l].astype(BF16).reshape(n_eb, PEER_EXPERT_BLOCK, d).transpose(0, 2, 1)
        q, k, v, pin = _norm_proj(y_p, norm_mix[l], w_in16, 512)
        q3, k3, v3, pin3 = (a.reshape(bp, sp, aw) for a in (q, k, v, pin))
        attn = _moba_prompt(q3, k3, v3)
        pool = _pool_mix(pin3, jnp.zeros((bp, POOL_PREFIX, aw), F32), w_pool16, pool_scale[l], 0)
        y_p = _ffn(y_p, attn.reshape(bp * sp, aw), pool.reshape(bp * sp, aw), wo16, norm_ffn[l], wq16,
                   peer_keys[l], u16, vt16, norm_final, l == depth - 1)
        kp.append(k3.reshape(bp, sp, n_heads, HEAD_DIM))
        vp.append(v3.reshape(bp, sp, n_heads, HEAD_DIM))
        pp.append(pin3[:, sp - pool_buf:])
        q, k, v, pin = _norm_proj(y_s, norm_mix[l], w_in16, db * ts)
        q3, k3, v3, pin3 = (a.reshape(db, ts, aw) for a in (q, k, v, pin))
        attn = _moba_sample(q3, k3, v3, ck, cv, page_table, l)
        prefix = jnp.concatenate([jnp.zeros((db, 1, aw), F32), state_pool[l]], axis=1)
        pin_pad = jnp.pad(pin3, ((0, 0), (0, ts_pad - ts), (0, 0)))
        pool = _pool_mix(pin_pad, prefix, w_pool16, pool_scale[l], past_len)[:, :ts]
        y_s = _ffn(y_s, attn.reshape(db * ts, aw), pool.reshape(db * ts, aw), wo16, norm_ffn[l], wq16,
                   peer_keys[l], u16, vt16, norm_final, l == depth - 1)
        kss.append(k3.reshape(db, ts, n_heads, HEAD_DIM))
        vss.append(v3.reshape(db, ts, n_heads, HEAD_DIM))
        pss.append(jnp.concatenate([state_pool[l], pin3], axis=1)[:, ts:])
    y_prompt = y_p.reshape(bp, sp, d)
    y_sample = y_s.reshape(db, ts, d)
    return (y_prompt, y_sample, jnp.stack(kp, axis=1), jnp.stack(vp, axis=1), jnp.stack(pp, axis=0),
            jnp.stack(kss, axis=1), jnp.stack(vss, axis=1), jnp.stack(pss, axis=0))
```

```python
import functools

import jax
import jax.numpy as jnp
from jax import lax
from jax.experimental import pallas as pl
from jax.experimental.pallas import tpu as pltpu

F32 = jnp.float32
BF16 = jnp.bfloat16

HEAD_DIM = 64
MOBA_BLOCK = 256
MOBA_TOPK = 3
POOL_WINDOWS = (2, 4, 8, 16)
POOL_PREFIX = 16
PEER_HEADS = 8
PEER_TOPK = 16
RMS_EPS = 1e-6
NEG = -1e30

LANES = 128
VMEM_LIMIT = 56 * 1024 * 1024

_NT = (((1,), (1,)), ((), ()))


def _params(semantics):
    return pltpu.CompilerParams(dimension_semantics=semantics, vmem_limit_bytes=VMEM_LIMIT)


def _rms(x, g):
    return x * lax.rsqrt(jnp.mean(x * x, axis=-1, keepdims=True) + RMS_EPS) * g


def _split_bf16(x):
    hi = x.astype(BF16)
    lo = (x - hi.astype(F32)).astype(BF16)
    return hi, lo


def _dot_nt_3pass(a, b):
    a_hi, a_lo = _split_bf16(a)
    b_hi, b_lo = _split_bf16(b)
    d = functools.partial(lax.dot_general, dimension_numbers=_NT, preferred_element_type=F32)
    return d(a_hi, b_hi) + (d(a_hi, b_lo) + d(a_lo, b_hi))


def _norm_proj_kernel(x_ref, g_ref, w_ref, q_ref, k_ref, v_ref, p_ref):
    h = _rms(x_ref[...], g_ref[...])
    proj = jnp.dot(h.astype(BF16), w_ref[...], preferred_element_type=F32)
    w = q_ref.shape[1]
    q_ref[...] = proj[:, 0 * w:1 * w]
    k_ref[...] = proj[:, 1 * w:2 * w]
    v_ref[...] = proj[:, 2 * w:3 * w]
    p_ref[...] = proj[:, 3 * w:4 * w]


def _norm_proj(x, g, w16, tm):
    t, d = x.shape
    w = w16.shape[1] // 4
    row = pl.BlockSpec((tm, d), lambda i: (i, 0))
    out = pl.BlockSpec((tm, w), lambda i: (i, 0))
    return pl.pallas_call(
        _norm_proj_kernel,
        grid=(t // tm,),
        in_specs=[row, pl.BlockSpec((1, d), lambda i: (0, 0)), pl.BlockSpec(w16.shape, lambda i: (0, 0))],
        out_specs=[out] * 4,
        out_shape=[jax.ShapeDtypeStruct((t, w), F32)] * 4,
        compiler_params=_params(("parallel",)),
        name="norm_proj",
    )(x, g.reshape(1, d), w16)


def _block_rank_select(gate, n_cand):
    lane = lax.broadcasted_iota(jnp.int32, gate.shape, 1)
    rank = jnp.zeros(gate.shape, F32)
    for m in range(n_cand):
        gm = gate[:, m:m + 1]
        beats = jnp.where(gm > gate, 1.0, jnp.where(gm == gate, jnp.where(lane > m, 1.0, 0.0), 0.0))
        rank = rank + beats
    return jnp.where(lane < n_cand, rank, float(MOBA_TOPK)) < float(MOBA_TOPK)


def _moba_prompt_kernel(q_ref, k_ref, v_ref, o_ref):
    s_len = q_ref.shape[1]
    nb = s_len // MOBA_BLOCK
    k = k_ref[0]
    k16 = k.astype(BF16)
    v16 = v_ref[0].astype(BF16)
    lane = lax.broadcasted_iota(jnp.int32, (1, LANES), 1)
    head_lanes = [lane < HEAD_DIM, lane >= HEAD_DIM]
    means = [jnp.mean(k[n * MOBA_BLOCK:(n + 1) * MOBA_BLOCK], axis=0, keepdims=True) for n in range(nb)]
    kbar = jnp.concatenate(means + [jnp.zeros((LANES - nb, LANES), F32)], axis=0)
    row = lax.broadcasted_iota(jnp.int32, (MOBA_BLOCK, MOBA_BLOCK), 0)
    col = lax.broadcasted_iota(jnp.int32, (MOBA_BLOCK, MOBA_BLOCK), 1)
    causal = row >= col
    scale = HEAD_DIM ** -0.5
    for i in range(nb):
        q = q_ref[0, i * MOBA_BLOCK:(i + 1) * MOBA_BLOCK, :]
        n_keys = (i + 1) * MOBA_BLOCK
        outs = []
        for hh in range(2):
            qh = jnp.where(head_lanes[hh], q, 0.0)
            s = lax.dot_general((qh * scale).astype(BF16), k16[:n_keys], _NT, preferred_element_type=F32)
            pieces = []
            if i > 0:
                gate = _dot_nt_3pass(qh, kbar)
                sel = jnp.where(_block_rank_select(gate, i), 1.0, 0.0)
                pieces = [jnp.where(sel[:, n:n + 1] > 0.5, s[:, n * MOBA_BLOCK:(n + 1) * MOBA_BLOCK], NEG)
                          for n in range(i)]
            pieces.append(jnp.where(causal, s[:, i * MOBA_BLOCK:], NEG))
            s = jnp.concatenate(pieces, axis=1)
            m = jnp.max(s, axis=-1, keepdims=True)
            p = jnp.exp(s - m)
            l = jnp.sum(p, axis=-1, keepdims=True)
            o = jnp.dot(p.astype(BF16), v16[:n_keys], preferred_element_type=F32)
            outs.append(o / l)
        o_ref[0, i * MOBA_BLOCK:(i + 1) * MOBA_BLOCK, :] = jnp.where(head_lanes[0], outs[0], outs[1])


def _moba_prompt(q, k, v):
    b, s, w = q.shape
    spec = pl.BlockSpec((1, s, LANES), lambda bi, pi: (bi, 0, pi))
    return pl.pallas_call(
        _moba_prompt_kernel,
        grid=(b, w // LANES),
        in_specs=[spec] * 3,
        out_specs=spec,
        out_shape=jax.ShapeDtypeStruct((b, s, w), F32),
        compiler_params=_params(("parallel", "parallel")),
        name="moba_prompt",
    )(q, k, v)


def _moba_sample_kernel(pt_ref, q_ref, kn_ref, vn_ref, *refs, blocks_per_step, pages_per_block):
    del pt_ref
    n_pg = blocks_per_step * pages_per_block
    kt_refs, vt_refs = refs[:n_pg], refs[n_pg:2 * n_pg]
    o_ref, m_scr, l_scr, g_scr, o_scr = refs[2 * n_pg:]
    step = pl.program_id(1)
    tq, w = q_ref.shape[1], q_ref.shape[2]
    n_heads = w // HEAD_DIM
    rows = tq * n_heads
    scale = HEAD_DIM ** -0.5

    head_of_lane = lax.broadcasted_iota(jnp.int32, (n_heads, w), 1) // HEAD_DIM
    head_mask = jnp.where(head_of_lane == lax.broadcasted_iota(jnp.int32, (n_heads, w), 0), 1.0, 0.0)
    q = q_ref[0]
    qbd = jnp.concatenate([jnp.broadcast_to(q[j:j + 1], (n_heads, w)) * head_mask for j in range(tq)], axis=0)

    @pl.when(step == 0)
    def _():
        m_scr[...] = jnp.zeros_like(m_scr)
        l_scr[...] = jnp.zeros_like(l_scr)
        g_scr[...] = jnp.zeros_like(g_scr)

    qs16 = (qbd * scale).astype(BF16)
    blk_lane = lax.broadcasted_iota(jnp.int32, (rows, LANES), 1)
    m_all, l_all, g_all = m_scr[...], l_scr[...], g_scr[...]
    for jj in range(blocks_per_step):
        n = step * blocks_per_step + jj
        pgs = range(jj * pages_per_block, (jj + 1) * pages_per_block)
        s = jnp.concatenate([jnp.dot(qs16, kt_refs[pg][0, 0].astype(BF16), preferred_element_type=F32)
                             for pg in pgs], axis=1)
        m_n = jnp.max(s, axis=-1, keepdims=True)
        p = jnp.exp(s - m_n)
        l_n = jnp.sum(p, axis=-1, keepdims=True)
        g_n = jnp.sum(s, axis=-1, keepdims=True)
        p16 = p.astype(BF16)
        page = kt_refs[0].shape[3]
        o_n = None
        for i, pg in enumerate(pgs):
            part = lax.dot_general(p16[:, i * page:(i + 1) * page], vt_refs[pg][0, 0].astype(BF16), _NT,
                                   preferred_element_type=F32)
            o_n = part if o_n is None else o_n + part
        o_scr[n] = o_n
        here = blk_lane == n
        m_all = jnp.where(here, m_n, m_all)
        l_all = jnp.where(here, l_n, l_all)
        g_all = jnp.where(here, g_n, g_all)
    m_scr[...] = m_all
    l_scr[...] = l_all
    g_scr[...] = g_all

    @pl.when(step == pl.num_programs(1) - 1)
    def _():
        n_past = o_scr.shape[0]
        sel = _block_rank_select(g_scr[...], n_past)
        m_all = m_scr[...]
        m_run = jnp.max(jnp.where(sel, m_all, NEG), axis=-1, keepdims=True)
        q_idx = lax.broadcasted_iota(jnp.int32, (rows, 1), 0) // n_heads
        kn = kn_ref[0]
        vn = vn_ref[0]
        s_own = []
        for j in range(tq):
            sj = jnp.sum(qbd * scale * kn[j:j + 1], axis=-1, keepdims=True)
            s_own.append(jnp.where(q_idx >= j, sj, NEG))
            m_run = jnp.maximum(m_run, s_own[j])
        wgt = jnp.where(sel, jnp.exp(m_all - m_run), 0.0)
        den = jnp.sum(wgt * l_scr[...], axis=-1, keepdims=True)
        num = jnp.zeros((rows, w), F32)
        for nn in range(n_past):
            num = num + wgt[:, nn:nn + 1] * o_scr[nn]
        for j in range(tq):
            e = jnp.where(q_idx >= j, jnp.exp(s_own[j] - m_run), 0.0)
            den = den + e
            num = num + e * vn[j:j + 1]
        acc = num / den
        out_rows = [jnp.sum(acc[j * n_heads:(j + 1) * n_heads] * head_mask, axis=0, keepdims=True)
                    for j in range(tq)]
        o_ref[0] = jnp.concatenate(out_rows, axis=0)


SAMPLE_BLOCKS_PER_STEP = 8


def _moba_sample(q, k_new, v_new, cache_kt, cache_vt, page_table, layer):
    db, tq, w = q.shape
    page = cache_kt.shape[3]
    n_pages = page_table.shape[1]
    ppb = MOBA_BLOCK // page
    assert MOBA_BLOCK == ppb * page and n_pages % (ppb * SAMPLE_BLOCKS_PER_STEP) == 0
    n_past = n_pages // ppb
    assert n_past <= LANES and tq <= MOBA_BLOCK
    rows = tq * (w // HEAD_DIM)
    n_pg = SAMPLE_BLOCKS_PER_STEP * ppb
    tok = pl.BlockSpec((1, tq, w), lambda b, s, pt: (b, 0, 0))

    def page_spec(which):
        return pl.BlockSpec((1, 1, w, page), lambda b, s, pt: (pt[b, s * n_pg + which], layer, 0, 0))

    pages = [page_spec(i) for i in range(n_pg)]
    grid_spec = pltpu.PrefetchScalarGridSpec(
        num_scalar_prefetch=1,
        grid=(db, n_past // SAMPLE_BLOCKS_PER_STEP),
        in_specs=[tok, tok, tok] + pages + pages,
        out_specs=tok,
        scratch_shapes=[pltpu.VMEM((rows, LANES), F32)] * 3 + [pltpu.VMEM((n_past, rows, w), F32)],
    )
    return pl.pallas_call(
        functools.partial(_moba_sample_kernel, blocks_per_step=SAMPLE_BLOCKS_PER_STEP, pages_per_block=ppb),
        grid_spec=grid_spec,
        out_shape=jax.ShapeDtypeStruct((db, tq, w), F32),
        compiler_params=_params(("parallel", "arbitrary")),
        name="moba_sample",
    )(page_table, q, k_new, v_new, *([cache_kt] * n_pg), *([cache_vt] * n_pg))


def _pool_mix_kernel(x_ref, pre_ref, w_ref, sc_ref, o_ref, xp_scr, *, pos0):
    t = x_ref.shape[1]
    gw = w_ref.shape[1]
    xp_scr[0:POOL_PREFIX, :] = pre_ref[0]
    xp_scr[POOL_PREFIX:POOL_PREFIX + t, :] = x_ref[0]
    pos = (lax.broadcasted_iota(jnp.int32, (t, 1), 0) + (pos0 + 1)).astype(F32)
    outs = []
    for g, win in enumerate(POOL_WINDOWS):
        sl = slice(g * gw, (g + 1) * gw)
        cur = xp_scr[POOL_PREFIX:POOL_PREFIX + t, sl]
        acc = cur
        for j in range(1, win):
            acc = acc + xp_scr[POOL_PREFIX - j:POOL_PREFIX - j + t, sl]
        r = acc / jnp.minimum(pos, float(win)) - cur
        outs.append(jnp.dot(r.astype(BF16), w_ref[g], preferred_element_type=F32))
    o_ref[0] = jnp.concatenate(outs, axis=1) * sc_ref[...]


def _pool_mix(pin, prefix, w_pool16, scale, pos0):
    b, t, w = pin.shape
    seq = pl.BlockSpec((1, t, w), lambda i: (i, 0, 0))
    return pl.pallas_call(
        functools.partial(_pool_mix_kernel, pos0=pos0),
        grid=(b,),
        in_specs=[seq, pl.BlockSpec((1, POOL_PREFIX, w), lambda i: (i, 0, 0)),
                  pl.BlockSpec(w_pool16.shape, lambda i: (0, 0, 0)), pl.BlockSpec((1, w), lambda i: (0, 0))],
        out_specs=seq,
        out_shape=jax.ShapeDtypeStruct((b, t, w), F32),
        scratch_shapes=[pltpu.VMEM((POOL_PREFIX + t, w), F32)],
        compiler_params=_params(("parallel",)),
        name="pool_mix",
    )(pin, prefix, w_pool16, scale.reshape(1, w))


def _mix_out_kernel(x_ref, a_ref, p_ref, wo_ref, g_ref, wq_ref, keys_ref, x1_ref, ht_ref, e1_ref, e2_ref, thr_ref):
    aw = a_ref.shape[1]
    mix = (jnp.dot(a_ref[...].astype(BF16), wo_ref[0:aw, :], preferred_element_type=F32)
           + jnp.dot(p_ref[...].astype(BF16), wo_ref[aw:, :], preferred_element_type=F32))
    x1 = x_ref[...] + mix
    x1_ref[...] = x1
    h = _rms(x1, g_ref[...])
    h16 = h.astype(BF16)
    ht_ref[...] = h.T.astype(BF16)
    qq = jnp.dot(h16, wq_ref[...], preferred_element_type=F32)
    half = keys_ref.shape[2]
    for hd in range(PEER_HEADS):
        q1 = qq[:, (2 * hd) * half:(2 * hd + 1) * half]
        q2 = qq[:, (2 * hd + 1) * half:(2 * hd + 2) * half]
        s1 = _dot_nt_3pass(keys_ref[0], q1)
        s2 = _dot_nt_3pass(keys_ref[1], q2)
        e1_ref[hd], e2_ref[hd], thr_ref[hd] = _gate_factors(s1, s2)


def _mix_out(x, attn, pool, wo16, g, wq16, keys, tm):
    t, d = x.shape
    aw = attn.shape[1]
    nk = keys.shape[1]
    row = lambda wdt: pl.BlockSpec((tm, wdt), lambda i: (i, 0))
    full = lambda a: pl.BlockSpec(a.shape, lambda i: (0,) * a.ndim)
    sc = pl.BlockSpec((PEER_HEADS, nk, tm), lambda i: (0, 0, i))
    return pl.pallas_call(
        _mix_out_kernel,
        grid=(t // tm,),
        in_specs=[row(d), row(aw), row(aw), full(wo16), pl.BlockSpec((1, d), lambda i: (0, 0)), full(wq16), full(keys)],
        out_specs=[row(d), pl.BlockSpec((d, tm), lambda i: (0, i)), sc, sc,
                   pl.BlockSpec((PEER_HEADS, 1, tm), lambda i: (0, 0, i))],
        out_shape=[jax.ShapeDtypeStruct((t, d), F32), jax.ShapeDtypeStruct((d, t), BF16),
                   jax.ShapeDtypeStruct((PEER_HEADS, nk, t), F32), jax.ShapeDtypeStruct((PEER_HEADS, nk, t), BF16),
                   jax.ShapeDtypeStruct((PEER_HEADS, 1, t), F32)],
        compiler_params=_params(("parallel",)),
        name="mix_out",
    )(x, attn, pool, wo16, g.reshape(1, d), wq16, keys)


def _top_values(s, count):
    vals = []
    cur = s
    for _ in range(count):
        mx = jnp.max(cur, axis=0, keepdims=True)
        vals.append(mx)
        cur = jnp.where(cur == mx, NEG, cur)
    return vals


def _compare_exchange(xs, i, j, descending=True):
    hi, lo = jnp.maximum(xs[i], xs[j]), jnp.minimum(xs[i], xs[j])
    xs[i], xs[j] = (hi, lo) if descending else (lo, hi)


def _bitonic_merge(xs):
    n = len(xs)
    j = n // 2
    while j >= 1:
        for i in range(n):
            if i ^ j > i:
                _compare_exchange(xs, i, i ^ j)
        j //= 2


def _top_sorted(s, count):
    sub = s.shape[0] // count
    xs = [s[k * sub:(k + 1) * sub] for k in range(count)]
    k = 2
    while k <= count:
        j = k // 2
        while j >= 1:
            for i in range(count):
                if i ^ j > i:
                    _compare_exchange(xs, i, i ^ j, descending=(i & k) == 0)
            j //= 2
        k *= 2
    shift = sub // 2
    while shift >= 1:
        ys = [pltpu.roll(x, shift, 0) for x in xs]
        xs = [jnp.maximum(xs[i], ys[count - 1 - i]) for i in range(count)]
        _bitonic_merge(xs)
        shift //= 2
    return [x[0:1] for x in xs]


def _gate_factors(s1, s2):
    v1 = _top_sorted(s1, PEER_TOPK)
    v2 = _top_sorted(s2, PEER_TOPK)
    pairs = [(a, b) for a in range(PEER_TOPK) for b in range(PEER_TOPK // (a + 1))]
    pad = [jnp.full_like(v1[0], NEG)] * (-len(pairs) % 8)
    cand = jnp.concatenate([v1[a] + v2[b] for a, b in pairs] + pad, axis=0)
    tau = _top_values(cand, PEER_TOPK)[-1]
    cmax = v1[0] + v2[0]
    chosen = cand >= tau
    z = jnp.sum(jnp.where(chosen, jnp.exp(cand - cmax), 0.0), axis=0, keepdims=True)
    e1 = lambda s: (jnp.exp(s - v1[0]) / z).astype(BF16)
    e2 = lambda s: jnp.exp(s - v2[0]).astype(BF16)
    e1_dense = jnp.where(s1 >= v1[-1], e1(s1).astype(F32), 0.0)
    e2_dense = jnp.where(s2 >= v2[-1], e2(s2), jnp.zeros_like(s2, BF16))
    v2m = jnp.concatenate(v2, axis=0)
    e2m = e2(v2m)
    thr = None
    for a in range(PEER_TOPK):
        prod = (e1(jnp.broadcast_to(v1[a], v2m.shape)) * e2m).astype(F32)
        low = jnp.min(jnp.where(v1[a] + v2m >= tau, prod, 1.0), axis=0, keepdims=True)
        thr = low if thr is None else jnp.minimum(thr, low)
    return e1_dense, e2_dense, thr


PEER_SUB_BLOCK = 512
PEER_PRE_BLOCK = 1024


def _peer_dense_kernel(ht_ref, x_ref, u_ref, vt_ref, e1_ref, e2_ref, thr_ref, gf_ref, y_ref,
                       a_scr, w_scr, acc_scr, *, final_norm):
    j = pl.program_id(1)
    tt = ht_ref.shape[1]
    nk = e2_ref.shape[1]
    eb = u_ref.shape[0]

    @pl.when(j == 0)
    def _():
        acc_scr[...] = jnp.zeros_like(acc_scr)

    ht = ht_ref[...]
    zero = jnp.zeros((nk, LANES), BF16)
    n_sub = eb // PEER_SUB_BLOCK

    subs_per_pre = PEER_PRE_BLOCK // PEER_SUB_BLOCK

    def pre_activations(pb):
        blk = slice(pb * PEER_PRE_BLOCK, (pb + 1) * PEER_PRE_BLOCK)
        a_scr[blk, :] = jnp.dot(u_ref[blk, :], ht, preferred_element_type=F32)

    def gated_weights(sb):
        for c in range(sb * PEER_SUB_BLOCK // nk, (sb + 1) * PEER_SUB_BLOCK // nk):
            rows = slice(c * nk, (c + 1) * nk)
            for lt in range(tt // LANES):
                ls = slice(lt * LANES, (lt + 1) * LANES)
                gate = None
                for hd in range(PEER_HEADS):
                    e1 = jnp.broadcast_to(e1_ref[hd, c:c + 1, ls], (nk, LANES)).astype(BF16)
                    thr = jnp.broadcast_to(thr_ref[hd, :, ls], (nk, LANES)).astype(BF16)
                    p = e1 * e2_ref[hd, :, ls]
                    g = jnp.where(p >= thr, p, zero)
                    gate = g if gate is None else gate + g
                a = a_scr[rows, ls]
                gelu = 0.5 * a * (1.0 + lax.erf(a * (2.0 ** -0.5)))
                w_scr[rows, ls] = gate * gelu.astype(BF16)

    pre_activations(0)
    for sb in range(n_sub):
        if (sb + 1) % subs_per_pre == 0 and sb + 1 < n_sub:
            pre_activations((sb + 1) // subs_per_pre)
        gated_weights(sb)
        sub = slice(sb * PEER_SUB_BLOCK, (sb + 1) * PEER_SUB_BLOCK)
        acc_scr[...] += jnp.dot(vt_ref[0, :, sub], w_scr[sub, :], preferred_element_type=F32)

    @pl.when(j == pl.num_programs(1) - 1)
    def _():
        y = x_ref[...] + acc_scr[...].T
        y_ref[...] = _rms(y, gf_ref[...]) if final_norm else y


PEER_EXPERT_BLOCK = 2048


def _peer_dense(ht16, x, u16, vt16, e1t, e2t, thr, g_final, final_norm, tt):
    t, d = x.shape
    e = u16.shape[0]
    eb = PEER_EXPERT_BLOCK
    nh, nk, _ = e1t.shape
    assert eb % PEER_SUB_BLOCK == 0 and PEER_SUB_BLOCK % nk == 0 and vt16.shape == (e // eb, d, eb)
    tok = lambda: pl.BlockSpec((tt, d), lambda i, j: (i, 0))
    return pl.pallas_call(
        functools.partial(_peer_dense_kernel, final_norm=final_norm),
        grid=(t // tt, e // eb),
        in_specs=[pl.BlockSpec((d, tt), lambda i, j: (0, i)), tok(),
                  pl.BlockSpec((eb, d), lambda i, j: (j, 0)),
                  pl.BlockSpec((1, d, eb), lambda i, j: (j, 0, 0)),
                  pl.BlockSpec((nh, eb // nk, tt), lambda i, j: (0, j, i)),
                  pl.BlockSpec((nh, nk, tt), lambda i, j: (0, 0, i)),
                  pl.BlockSpec((nh, 1, tt), lambda i, j: (0, 0, i)),
                  pl.BlockSpec((1, d), lambda i, j: (0, 0))],
        out_specs=tok(),
        out_shape=jax.ShapeDtypeStruct((t, d), F32),
        scratch_shapes=[pltpu.VMEM((eb, tt), F32), pltpu.VMEM((eb, tt), BF16), pltpu.VMEM((d, tt), F32)],
        compiler_params=_params(("parallel", "arbitrary")),
        name="peer_dense",
    )(ht16, x, u16, vt16, e1t, e2t, thr, g_final.reshape(1, d))


def _token_tile(t, want):
    return want if t % want == 0 else t


def _ffn(x, attn, pool, wo16, g_ffn, wq16, keys, u16, vt16, g_final, final_norm):
    t = x.shape[0]
    x1, ht16, e1t, e2t, thr = _mix_out(x, attn, pool, wo16, g_ffn, wq16, keys, _token_tile(t, 512))
    return _peer_dense(ht16, x1, u16, vt16, e1t, e2t, thr, g_final, final_norm, _token_tile(t, 512))


def kernel(x_prompt, x_sample, cache_k, cache_v, state_pool, page_table, norm_mix, w_in, w_pool, pool_scale,
           w_out, norm_ffn, peer_wq, peer_keys, peer_u, peer_v, norm_final):
    bp, sp, d = x_prompt.shape
    db, ts, _ = x_sample.shape
    depth = w_in.shape[0]
    aw = w_in.shape[2] // 4
    n_heads = aw // HEAD_DIM
    pool_buf = state_pool.shape[2]
    assert pool_buf == POOL_PREFIX - 1 and sp % MOBA_BLOCK == 0 and aw % LANES == 0
    n_pool, _, page, _, _ = cache_k.shape
    ck = cache_k.transpose(0, 1, 3, 4, 2).reshape(n_pool, depth, aw, page)
    cv = cache_v.transpose(0, 1, 3, 4, 2).reshape(n_pool, depth, aw, page)
    past_len = page_table.shape[1] * page
    ts_pad = -(-ts // 8) * 8

    y_p = x_prompt.reshape(bp * sp, d)
    y_s = x_sample.reshape(db * ts, d)
    kp, vp, pp, kss, vss, pss = [], [], [], [], [], []
    for l in range(depth):
        w_in16 = w_in[l].astype(BF16)
        w_pool16 = w_pool[l].astype(BF16)
        wo16 = w_out[l].astype(BF16)
        wq16 = peer_wq[l].astype(BF16)
        u16 = peer_u[l].astype(BF16)
        n_eb = peer_v.shape[1] // PEER_EXPERT_BLOCK
        vt16 = peer_v[l].astype(BF16).reshape(n_eb, PEER_EXPERT_BLOCK, d).transpose(0, 2, 1)
        q, k, v, pin = _norm_proj(y_p, norm_mix[l], w_in16, 512)
        q3, k3, v3, pin3 = (a.reshape(bp, sp, aw) for a in (q, k, v, pin))
        attn = _moba_prompt(q3, k3, v3)
        pool = _pool_mix(pin3, jnp.zeros((bp, POOL_PREFIX, aw), F32), w_pool16, pool_scale[l], 0)
        y_p = _ffn(y_p, attn.reshape(bp * sp, aw), pool.reshape(bp * sp, aw), wo16, norm_ffn[l], wq16,
                   peer_keys[l], u16, vt16, norm_final, l == depth - 1)
        kp.append(k3.reshape(bp, sp, n_heads, HEAD_DIM))
        vp.append(v3.reshape(bp, sp, n_heads, HEAD_DIM))
        pp.append(pin3[:, sp - pool_buf:])
        q, k, v, pin = _norm_proj(y_s, norm_mix[l], w_in16, db * ts)
        q3, k3, v3, pin3 = (a.reshape(db, ts, aw) for a in (q, k, v, pin))
        attn = _moba_sample(q3, k3, v3, ck, cv, page_table, l)
        prefix = jnp.concatenate([jnp.zeros((db, 1, aw), F32), state_pool[l]], axis=1)
        pin_pad = jnp.pad(pin3, ((0, 0), (0, ts_pad - ts), (0, 0)))
        pool = _pool_mix(pin_pad, prefix, w_pool16, pool_scale[l], past_len)[:, :ts]
        y_s = _ffn(y_s, attn.reshape(db * ts, aw), pool.reshape(db * ts, aw), wo16, norm_ffn[l], wq16,
                   peer_keys[l], u16, vt16, norm_final, l == depth - 1)
        kss.append(k3.reshape(db, ts, n_heads, HEAD_DIM))
        vss.append(v3.reshape(db, ts, n_heads, HEAD_DIM))
        pss.append(jnp.concatenate([state_pool[l], pin3], axis=1)[:, ts:])
    y_prompt = y_p.reshape(bp, sp, d)
    y_sample = y_s.reshape(db, ts, d)
    return (y_prompt, y_sample, jnp.stack(kp, axis=1), jnp.stack(vp, axis=1), jnp.stack(pp, axis=0),
            jnp.stack(kss, axis=1), jnp.stack(vss, axis=1), jnp.stack(pss, axis=0))
```
